```python
import math
import jax, jax.numpy as jnp
from jax import lax
import numpy as np

D_MODEL = 1024
BATCH = 8
SEQ = 4096
DEPTH = 2

CHUNK = 64
N_MEM = 256
MEM_HEADS = 4
MEM_HEAD_DIM = D_MODEL // MEM_HEADS
D_MIX = D_MODEL
POOL_WINDOWS = (2, 4, 8, 16)
POOL_WIDTH = D_MIX // 4
POOL_GROUP = POOL_WIDTH // len(POOL_WINDOWS)
QK_NOPE = 128
QK_ROPE = 64
V_HEAD = 128
MLA_HEADS = (D_MIX - POOL_WIDTH) // V_HEAD
Q_LORA = 256
KV_LORA = 128
ROPE_BASE = 10000.0
D_FF = 2816
Q_BLOCK = 128
D_IN = POOL_WIDTH + Q_LORA + KV_LORA + QK_ROPE
ALPHA = (2 * DEPTH) ** 0.25
BETA = (8 * DEPTH) ** -0.25
LN_EPS = 1e-5
RMS_EPS = 1e-6
NEG_INF = -1e30

kernel_name = 'hybrid_pool_mla_macaron_deepnorm'


def layer_norm(x, g, b):
    xf = x.astype(jnp.float32)
    mu = jnp.mean(xf, axis=-1, keepdims=True)
    var = jnp.mean(jnp.square(xf - mu), axis=-1, keepdims=True)
    y = (xf - mu) * lax.rsqrt(var + LN_EPS) * g.astype(jnp.float32) + b.astype(jnp.float32)
    return y.astype(x.dtype)


def rms_norm(x, g):
    xf = x.astype(jnp.float32)
    y = xf * lax.rsqrt(jnp.mean(jnp.square(xf), axis=-1, keepdims=True) + RMS_EPS)
    return (y * g.astype(jnp.float32)).astype(x.dtype)


def swiglu(x, w13, w2):
    gate, up = jnp.split(x @ w13, 2, axis=-1)
    return (jax.nn.silu(gate) * up) @ w2


def rope(x, positions):
    half = QK_ROPE // 2
    inv_freq = ROPE_BASE ** (-jnp.arange(half, dtype=jnp.float32) / half)
    ang = positions.astype(jnp.float32)[..., None] * inv_freq
    ang = ang.reshape(ang.shape[:2] + (1,) * (x.ndim - 3) + (half,))
    cos, sin = jnp.cos(ang), jnp.sin(ang)
    xf = x.astype(jnp.float32)
    x1, x2 = xf[..., :half], xf[..., half:]
    out = jnp.concatenate([x1 * cos - x2 * sin, x2 * cos + x1 * sin], axis=-1)
    return out.astype(x.dtype)


def pool_mixer(u, pool_w, pool_scale):
    B, S, _ = u.shape
    uf = u.astype(jnp.float32)
    cs = jnp.cumsum(uf, axis=1)
    t = jnp.arange(S)
    means = []
    for g, w in enumerate(POOL_WINDOWS):
        csg = cs[..., g * POOL_GROUP:(g + 1) * POOL_GROUP]
        prev = jnp.pad(csg[:, :S - w], ((0, 0), (w, 0), (0, 0)))
        cnt = jnp.minimum(t + 1, w).astype(jnp.float32)[None, :, None]
        means.append((csg - prev) / cnt)
    d = (jnp.concatenate(means, axis=-1) - uf).astype(u.dtype)
    d = d.reshape(B, S, len(POOL_WINDOWS), POOL_GROUP)
    y = jnp.einsum('bsgc,gcd->bsgd', d, pool_w).reshape(B, S, POOL_WIDTH)
    return y * pool_scale


def mla_mixer(c_q, c_kv, k_pe, positions, q_norm_g, w_uq, kv_norm_g, w_ukv):
    B, S, _ = c_q.shape
    H = MLA_HEADS
    q = (rms_norm(c_q, q_norm_g) @ w_uq).reshape(B, S, H, QK_NOPE + QK_ROPE)
    q = jnp.concatenate([q[..., :QK_NOPE], rope(q[..., QK_NOPE:], positions)], axis=-1)
    kv = (rms_norm(c_kv, kv_norm_g) @ w_ukv).reshape(B, S, H, QK_NOPE + V_HEAD)
    k_nope, v = kv[..., :QK_NOPE], kv[..., QK_NOPE:]
    k_rot = rope(k_pe, positions)
    k = jnp.concatenate([k_nope, jnp.broadcast_to(k_rot[:, :, None, :], (B, S, H, QK_ROPE))], axis=-1)
    scale = (QK_NOPE + QK_ROPE) ** -0.5
    nb = S // Q_BLOCK
    q_blocks = q.reshape(B, nb, Q_BLOCK, H, QK_NOPE + QK_ROPE).transpose(1, 0, 2, 3, 4)
    q_idx = jnp.arange(S).reshape(nb, Q_BLOCK)
    k_chunk = jnp.arange(S) // CHUNK

    def attend(args):
        qb, qi = args
        s = jnp.einsum('bqhd,bkhd->bhqk', qb, k, preferred_element_type=jnp.float32) * scale
        mask = (qi[:, None] // CHUNK) >= k_chunk[None, :]
        s = jnp.where(mask[None, None], s, NEG_INF)
        p = jax.nn.softmax(s, axis=-1).astype(v.dtype)
        return jnp.einsum('bhqk,bkhd->bqhd', p, v)

    o = lax.map(attend, (q_blocks, q_idx))
    return o.transpose(1, 0, 2, 3, 4).reshape(B, S, H * V_HEAD)


def memory_cross_attention(x, mem, wq, wkv, wo):
    B, S, _ = x.shape
    q = (x @ wq).reshape(B, S, MEM_HEADS, MEM_HEAD_DIM)
    k, v = jnp.split(mem @ wkv, 2, axis=-1)
    k = k.reshape(B, mem.shape[1], MEM_HEADS, MEM_HEAD_DIM)
    v = v.reshape(B, mem.shape[1], MEM_HEADS, MEM_HEAD_DIM)
    s = jnp.einsum('bshd,bmhd->bhsm', q, k, preferred_element_type=jnp.float32) * MEM_HEAD_DIM ** -0.5
    p = jax.nn.softmax(s, axis=-1).astype(v.dtype)
    o = jnp.einsum('bhsm,bmhd->bshd', p, v).reshape(B, S, D_MODEL)
    return o @ wo


def setup_inputs(seed: int = 0) -> dict:
    key = jax.random.key(seed)
    ks = jax.random.split(key, 24)

    def nrm(k, shape, scale):
        return jax.random.normal(k, shape, jnp.float32) * scale

    x = nrm(ks[0], (BATCH, SEQ, D_MODEL), 1.0)
    mem = nrm(ks[1], (BATCH, N_MEM, D_MODEL), 1.0)
    start = jax.random.randint(ks[2], (BATCH, 1), 0, 8192, dtype=jnp.int32)
    positions = (start + jnp.arange(SEQ, dtype=jnp.int32)[None, :]).astype(jnp.int32)
    L = DEPTH
    return {
        'x': x,
        'mem': mem,
        'positions': positions,
        'ln_g': 1.0 + nrm(ks[3], (L, 4, D_MODEL), 0.05),
        'ln_b': nrm(ks[4], (L, 4, D_MODEL), 0.02),
        'ffn1_w13': nrm(ks[5], (L, D_MODEL, 2 * D_FF), D_MODEL ** -0.5),
        'ffn1_w2': nrm(ks[6], (L, D_FF, D_MODEL), BETA * D_FF ** -0.5),
        'w_in': nrm(ks[7], (L, D_MODEL, D_IN), D_MODEL ** -0.5),
        'pool_w': nrm(ks[8], (L, len(POOL_WINDOWS), POOL_GROUP, POOL_GROUP), POOL_GROUP ** -0.5),
        'pool_scale': 1.0 + nrm(ks[9], (L, POOL_WIDTH), 0.1),
        'q_norm_g': 1.0 + nrm(ks[10], (L, Q_LORA), 0.05),
        'w_uq': nrm(ks[11], (L, Q_LORA, MLA_HEADS * (QK_NOPE + QK_ROPE)), Q_LORA ** -0.5),
        'kv_norm_g': 1.0 + nrm(ks[12], (L, KV_LORA), 0.05),
        'w_ukv': nrm(ks[13], (L, KV_LORA, MLA_HEADS * (QK_NOPE + V_HEAD)), KV_LORA ** -0.5),
        'w_out': nrm(ks[14], (L, D_MIX, D_MODEL), BETA * D_MIX ** -0.5),
        'mem_wq': nrm(ks[15], (L, D_MODEL, D_MODEL), D_MODEL ** -0.5),
        'mem_wkv': nrm(ks[16], (L, D_MODEL, 2 * D_MODEL), D_MODEL ** -0.5),
        'mem_wo': nrm(ks[17], (L, D_MODEL, D_MODEL), BETA * D_MODEL ** -0.5),
        'ffn2_w13': nrm(ks[18], (L, D_MODEL, 2 * D_FF), D_MODEL ** -0.5),
        'ffn2_w2': nrm(ks[19], (L, D_FF, D_MODEL), BETA * D_FF ** -0.5),
    }


def reference(x, mem, positions, ln_g, ln_b, ffn1_w13, ffn1_w2, w_in, pool_w, pool_scale,
              q_norm_g, w_uq, kv_norm_g, w_ukv, w_out, mem_wq, mem_wkv, mem_wo,
              ffn2_w13, ffn2_w2):
    for l in range(DEPTH):
        x = layer_norm(ALPHA * x + 0.5 * swiglu(x, ffn1_w13[l], ffn1_w2[l]), ln_g[l, 0], ln_b[l, 0])
        h = x @ w_in[l]
        o0 = POOL_WIDTH
        o1 = o0 + Q_LORA
        o2 = o1 + KV_LORA
        u_pool, c_q, c_kv, k_pe = h[..., :o0], h[..., o0:o1], h[..., o1:o2], h[..., o2:]
        y_pool = pool_mixer(u_pool, pool_w[l], pool_scale[l])
        y_mla = mla_mixer(c_q, c_kv, k_pe, positions, q_norm_g[l], w_uq[l],
                          kv_norm_g[l], w_ukv[l])
        y_mix = jnp.concatenate([y_pool, y_mla], axis=-1) @ w_out[l]
        x = layer_norm(ALPHA * x + y_mix, ln_g[l, 1], ln_b[l, 1])
        y_mem = memory_cross_attention(x, mem, mem_wq[l], mem_wkv[l], mem_wo[l])
        x = layer_norm(ALPHA * x + y_mem, ln_g[l, 2], ln_b[l, 2])
        x = layer_norm(ALPHA * x + 0.5 * swiglu(x, ffn2_w13[l], ffn2_w2[l]), ln_g[l, 3], ln_b[l, 3])
    return x
```

```python
import functools
import math

import jax
import jax.numpy as jnp
from jax import lax
from jax.experimental import pallas as pl
from jax.experimental.pallas import tpu as pltpu

D_MODEL = 1024
DEPTH = 2
CHUNK = 64
MEM_HEADS = 4
MEM_HEAD_DIM = D_MODEL // MEM_HEADS
POOL_WINDOWS = (2, 4, 8, 16)
POOL_WIDTH = 256
POOL_GROUP = POOL_WIDTH // len(POOL_WINDOWS)
POOL_HALO = 2 * POOL_WINDOWS[-1]
QK_NOPE = 128
QK_ROPE = 64
V_HEAD = 128
MLA_HEADS = 6
QK_HEAD = QK_NOPE + 2 * QK_ROPE
Q_LORA = 256
KV_LORA = 128
ROPE_BASE = 10000.0
D_FF = 2816
ALPHA = (2 * DEPTH) ** 0.25
LN_EPS = 1e-5
RMS_EPS = 1e-6
NEG_INF = -1e30

V7X_VMEM_BYTES = 64 * 1024 * 1024
VMEM_LIMIT = 56 * 1024 * 1024

FFN_ROWS = 512
FFN_CHUNK = 1408
ROW_TILE = 512
ATT_BLOCK = 512

BF16 = jnp.bfloat16
F32 = jnp.float32


def _dot(a, b):
    return jnp.dot(a, b, preferred_element_type=F32)


def _dot_nt(a, b):
    return lax.dot_general(a, b, (((1,), (1,)), ((), ())), preferred_element_type=F32)


def _layer_norm(z, g, b):
    mu = jnp.mean(z, axis=-1, keepdims=True)
    zc = z - mu
    var = jnp.mean(zc * zc, axis=-1, keepdims=True)
    return zc * lax.rsqrt(var + LN_EPS) * g + b


def _rms_norm(z, g):
    return z * lax.rsqrt(jnp.mean(z * z, axis=-1, keepdims=True) + RMS_EPS) * g


def _const_spec(shape):
    zeros = (0,) * len(shape)
    return pl.BlockSpec(shape, lambda *_: zeros, pipeline_mode=pl.Buffered(1))


def _params(*semantics):
    return pltpu.CompilerParams(dimension_semantics=semantics, vmem_limit_bytes=VMEM_LIMIT)


def _ffn_ln_kernel(x_ref, w13_ref, w2_ref, g_ref, b_ref, o_ref):
    x = x_ref[...]
    xb = x.astype(BF16)
    acc = None
    for c0 in range(0, D_FF, FFN_CHUNK):
        gate = _dot(xb, w13_ref[:, c0:c0 + FFN_CHUNK])
        up = _dot(xb, w13_ref[:, D_FF + c0:D_FF + c0 + FFN_CHUNK])
        act = (gate * jax.nn.sigmoid(gate) * up).astype(BF16)
        part = _dot(act, w2_ref[c0:c0 + FFN_CHUNK, :])
        acc = part if acc is None else acc + part
    o_ref[...] = _layer_norm(ALPHA * x + 0.5 * acc, g_ref[...], b_ref[...])


def _ffn_ln(x, w13, w2, g, b):
    m = x.shape[0]
    return pl.pallas_call(
        _ffn_ln_kernel,
        grid=(m // FFN_ROWS,),
        in_specs=[
            pl.BlockSpec((FFN_ROWS, D_MODEL), lambda i: (i, 0)),
            _const_spec(w13.shape),
            _const_spec(w2.shape),
            _const_spec(g.shape),
            _const_spec(b.shape),
        ],
        out_specs=pl.BlockSpec((FFN_ROWS, D_MODEL), lambda i: (i, 0)),
        out_shape=jax.ShapeDtypeStruct((m, D_MODEL), F32),
        compiler_params=_params("parallel"),
        name="ffn_ln",
    )(x, w13, w2, g, b)


def _rope_table_kernel(pos_ref, freq_ref, cos_ref, sin_ref):
    ang = pos_ref[...].astype(F32) * freq_ref[...]
    cos_ref[...] = jnp.cos(ang)
    sin_ref[...] = jnp.sin(ang)


def _rope_tables(positions):
    bsz, seq = positions.shape
    half = QK_ROPE // 2
    per_row = 128 // half
    inv_freq = ROPE_BASE ** (-jnp.arange(half, dtype=F32) / half)
    freq_row = jnp.tile(inv_freq, per_row)[None, :]
    pos_rows = jnp.repeat(positions.reshape(-1, per_row), half, axis=1)
    rows = pos_rows.shape[0]
    cos, sin = pl.pallas_call(
        _rope_table_kernel,
        grid=(1,),
        in_specs=[pl.BlockSpec((rows, 128), lambda i: (0, 0)),
                  pl.BlockSpec((1, 128), lambda i: (0, 0))],
        out_specs=[pl.BlockSpec((rows, 128), lambda i: (0, 0))] * 2,
        out_shape=[jax.ShapeDtypeStruct((rows, 128), F32)] * 2,
        compiler_params=_params("arbitrary"),
        name="rope_tables",
    )(pos_rows, freq_row)
    cos = cos.reshape(bsz, seq, half)
    sin = sin.reshape(bsz, seq, half)
    t_a = jnp.concatenate([cos, cos, -sin, sin], axis=-1)
    t_b = jnp.concatenate([-sin, sin, cos, cos], axis=-1)
    return t_a, t_b


def _mix_proj_kernel(x_ref, ta_ref, tb_ref, w_in_ref, pool_w_ref, pool_s_ref, qg_ref, w_uq_ref,
                     kvg_ref, w_ukv_ref, ypool_ref, q_ref, k_ref, v_ref, ext_ref):
    step = pl.program_id(1)
    rows = x_ref.shape[0]

    @pl.when(step == 0)
    def _():
        ext_ref[0:POOL_HALO, :] = jnp.zeros((POOL_HALO, POOL_WIDTH), F32)

    xb = x_ref[...].astype(BF16)
    h = _dot(xb, w_in_ref[...])
    u = h[:, :POOL_WIDTH]
    c_q = h[:, POOL_WIDTH:POOL_WIDTH + Q_LORA]
    o_kv = POOL_WIDTH + Q_LORA
    c_kv = h[:, o_kv:o_kv + KV_LORA]
    kpe_a = h[:, o_kv + KV_LORA:o_kv + KV_LORA + 128]
    kpe_b = h[:, o_kv + KV_LORA + 128:o_kv + KV_LORA + 256]

    half = POOL_HALO // 2
    ext_ref[POOL_HALO:, :] = u
    lane = lax.broadcasted_iota(jnp.int32, (1, POOL_WIDTH), 1)
    win_sum = None
    shift = 1
    while shift < POOL_WINDOWS[-1]:
        cur = ext_ref[half:, :] + ext_ref[pl.ds(half - shift, rows + half), :]
        shift *= 2
        gi = POOL_WINDOWS.index(shift)
        tile_sum = cur[half:, :]
        win_sum = tile_sum if win_sum is None else jnp.where(lane >= gi * POOL_GROUP, tile_sum, win_sum)
        if shift < POOL_WINDOWS[-1]:
            ext_ref[half:, :] = cur
    ext_ref[0:POOL_HALO, :] = u[rows - POOL_HALO:, :]

    t = step * rows + lax.broadcasted_iota(jnp.int32, (rows, 1), 0)
    win = jnp.left_shift(2, lane // POOL_GROUP)
    cnt = jnp.minimum(t + 1, win).astype(F32)
    d = (win_sum / cnt - u).astype(BF16)
    ypool_ref[...] = (_dot(d, pool_w_ref[...]) * pool_s_ref[...]).astype(BF16)

    t_a = ta_ref[...]
    t_b = tb_ref[...]
    scale = (QK_NOPE + QK_ROPE) ** -0.5
    qf = _dot(_rms_norm(c_q, qg_ref[...]).astype(BF16), w_uq_ref[...]) * scale
    kv = _dot(_rms_norm(c_kv, kvg_ref[...]).astype(BF16), w_ukv_ref[...])
    k_rot2 = (kpe_a * t_a + kpe_b * t_b).astype(BF16)
    for hd in range(MLA_HEADS):
        o = hd * QK_HEAD
        q_ref[:, o:o + QK_NOPE] = qf[:, o:o + QK_NOPE].astype(BF16)
        q_ref[:, o + QK_NOPE:o + QK_HEAD] = (qf[:, o + QK_NOPE:o + QK_HEAD] * t_a).astype(BF16)
        k_ref[:, o:o + QK_NOPE] = kv[:, o:o + QK_NOPE].astype(BF16)
        k_ref[:, o + QK_NOPE:o + QK_HEAD] = k_rot2
        v_ref[:, hd * V_HEAD:(hd + 1) * V_HEAD] = kv[:, o + QK_NOPE:o + QK_HEAD].astype(BF16)


def _mix_proj(x, t_a, t_b, w_in, pool_w, pool_s, qg, w_uq, kvg, w_ukv):
    bsz, seq, _ = x.shape
    rows = ROW_TILE
    tile = lambda width: pl.BlockSpec((None, rows, width), lambda b, i: (b, i, 0))
    return pl.pallas_call(
        _mix_proj_kernel,
        grid=(bsz, seq // rows),
        in_specs=[tile(D_MODEL), tile(128), tile(128),
                  _const_spec(w_in.shape), _const_spec(pool_w.shape), _const_spec(pool_s.shape),
                  _const_spec(qg.shape), _const_spec(w_uq.shape),
                  _const_spec(kvg.shape), _const_spec(w_ukv.shape)],
        out_specs=[tile(POOL_WIDTH), tile(MLA_HEADS * QK_HEAD), tile(MLA_HEADS * QK_HEAD),
                   tile(MLA_HEADS * V_HEAD)],
        out_shape=[jax.ShapeDtypeStruct((bsz, seq, POOL_WIDTH), BF16),
                   jax.ShapeDtypeStruct((bsz, seq, MLA_HEADS * QK_HEAD), BF16),
                   jax.ShapeDtypeStruct((bsz, seq, MLA_HEADS * QK_HEAD), BF16),
                   jax.ShapeDtypeStruct((bsz, seq, MLA_HEADS * V_HEAD), BF16)],
        scratch_shapes=[pltpu.VMEM((rows + POOL_HALO, POOL_WIDTH), F32)],
        compiler_params=_params("parallel", "arbitrary"),
        name="mix_proj",
    )(x, t_a, t_b, w_in, pool_w, pool_s, qg, w_uq, kvg, w_ukv)


def _mla_attn_kernel(q_ref, k_ref, v_ref, o_ref):
    blk = ATT_BLOCK
    qi = pl.program_id(2)
    q = q_ref[...]

    def step(j, carry, masked):
        m, l, acc = carry
        start = pl.multiple_of(j * blk, blk)
        s = _dot_nt(q, k_ref[pl.ds(start, blk), :])
        if masked:
            r = lax.broadcasted_iota(jnp.int32, (blk, blk), 0) // CHUNK
            c = lax.broadcasted_iota(jnp.int32, (blk, blk), 1) // CHUNK
            s = jnp.where(r >= c, s, NEG_INF)
        m_new = jnp.maximum(m, jnp.max(s, axis=-1, keepdims=True))
        alpha = jnp.exp(m - m_new)
        p = jnp.exp(s - m_new)
        l = alpha * l + jnp.sum(p, axis=-1, keepdims=True)
        acc = alpha * acc + _dot(p.astype(BF16), v_ref[pl.ds(start, blk), :])
        return m_new, l, acc

    init = (jnp.full((blk, 1), NEG_INF, F32), jnp.zeros((blk, 1), F32),
            jnp.zeros((blk, V_HEAD), F32))
    carry = lax.fori_loop(0, qi, functools.partial(step, masked=False), init)
    _, l, acc = step(qi, carry, masked=True)
    o_ref[...] = (acc / l).astype(BF16)


def _mla_attn(q, k, v):
    bsz, seq, _ = q.shape
    blk = ATT_BLOCK
    return pl.pallas_call(
        _mla_attn_kernel,
        grid=(bsz, MLA_HEADS, seq // blk),
        in_specs=[pl.BlockSpec((None, blk, QK_HEAD), lambda b, h, i: (b, i, h)),
                  pl.BlockSpec((None, seq, QK_HEAD), lambda b, h, i: (b, 0, h)),
                  pl.BlockSpec((None, seq, V_HEAD), lambda b, h, i: (b, 0, h))],
        out_specs=pl.BlockSpec((None, blk, V_HEAD), lambda b, h, i: (b, i, h)),
        out_shape=jax.ShapeDtypeStruct((bsz, seq, MLA_HEADS * V_HEAD), BF16),
        compiler_params=_params("parallel", "parallel", "arbitrary"),
        name="mla_attn",
    )(q, k, v)


def _out_proj_ln_kernel(x_ref, yp_ref, ym_ref, w_ref, g_ref, b_ref, o_ref):
    y = _dot(yp_ref[...], w_ref[0:POOL_WIDTH, :]) + _dot(ym_ref[...], w_ref[POOL_WIDTH:, :])
    o_ref[...] = _layer_norm(ALPHA * x_ref[...] + y, g_ref[...], b_ref[...])


def _out_proj_ln(x, y_pool, y_mla, w_out, g, b):
    m = x.shape[0]
    rows = ROW_TILE
    tile = lambda width: pl.BlockSpec((rows, width), lambda i: (i, 0))
    return pl.pallas_call(
        _out_proj_ln_kernel,
        grid=(m // rows,),
        in_specs=[tile(D_MODEL), tile(POOL_WIDTH), tile(MLA_HEADS * V_HEAD),
                  _const_spec(w_out.shape), _const_spec(g.shape), _const_spec(b.shape)],
        out_specs=tile(D_MODEL),
        out_shape=jax.ShapeDtypeStruct((m, D_MODEL), F32),
        compiler_params=_params("parallel"),
        name="out_proj_ln",
    )(x, y_pool, y_mla, w_out, g, b)


def _mem_kv_kernel(mem_ref, w_ref, k_ref, v_ref):
    kv = _dot(mem_ref[...].astype(BF16), w_ref[...])
    k_ref[...] = kv[:, :D_MODEL].astype(BF16)
    v_ref[...] = kv[:, D_MODEL:].astype(BF16)


def _mem_kv(mem, wkv):
    bsz, n_mem, _ = mem.shape
    blk = lambda: pl.BlockSpec((None, n_mem, D_MODEL), lambda b: (b, 0, 0))
    return pl.pallas_call(
        _mem_kv_kernel,
        grid=(bsz,),
        in_specs=[blk(), _const_spec(wkv.shape)],
        out_specs=[blk(), blk()],
        out_shape=[jax.ShapeDtypeStruct((bsz, n_mem, D_MODEL), BF16)] * 2,
        compiler_params=_params("parallel"),
        name="mem_kv",
    )(mem, wkv)


def _mem_attn_ln_kernel(x_ref, k_ref, v_ref, wq_ref, wo_ref, g_ref, b_ref, o_ref):
    x = x_ref[...]
    q = (_dot(x.astype(BF16), wq_ref[...]) * MEM_HEAD_DIM ** -0.5).astype(BF16)
    heads = []
    for hd in range(MEM_HEADS):
        sl = slice(hd * MEM_HEAD_DIM, (hd + 1) * MEM_HEAD_DIM)
        s = _dot_nt(q[:, sl], k_ref[:, sl])
        p = jnp.exp(s - jnp.max(s, axis=-1, keepdims=True))
        o = _dot(p.astype(BF16), v_ref[:, sl]) / jnp.sum(p, axis=-1, keepdims=True)
        heads.append(o.astype(BF16))
    y = _dot(jnp.concatenate(heads, axis=-1), wo_ref[...])
    o_ref[...] = _layer_norm(ALPHA * x + y, g_ref[...], b_ref[...])


def _mem_attn_ln(x, k_mem, v_mem, wq, wo, g, b):
    bsz, seq, _ = x.shape
    n_mem = k_mem.shape[1]
    rows = ROW_TILE
    tile = pl.BlockSpec((None, rows, D_MODEL), lambda bi, i: (bi, i, 0))
    mem_blk = lambda: pl.BlockSpec((None, n_mem, D_MODEL), lambda bi, i: (bi, 0, 0))
    return pl.pallas_call(
        _mem_attn_ln_kernel,
        grid=(bsz, seq // rows),
        in_specs=[tile, mem_blk(), mem_blk(), _const_spec(wq.shape), _const_spec(wo.shape),
                  _const_spec(g.shape), _const_spec(b.shape)],
        out_specs=tile,
        out_shape=jax.ShapeDtypeStruct((bsz, seq, D_MODEL), F32),
        compiler_params=_params("parallel", "arbitrary"),
        name="mem_attn_ln",
    )(x, k_mem, v_mem, wq, wo, g, b)


def _swap_halves(w):
    half = w.shape[-1] // 2
    return jnp.concatenate([w[..., half:], w[..., :half]], axis=-1)


def _w_in_layout(w_in):
    o_pe = POOL_WIDTH + Q_LORA + KV_LORA
    k_pe = w_in[:, o_pe:]
    k_sw = _swap_halves(k_pe)
    return jnp.concatenate([w_in[:, :o_pe], k_pe, k_sw, k_sw, k_pe], axis=-1).astype(BF16)


def _w_uq_layout(w_uq):
    w = w_uq.reshape(Q_LORA, MLA_HEADS, QK_NOPE + QK_ROPE)
    rope = w[..., QK_NOPE:]
    w = jnp.concatenate([w[..., :QK_NOPE], rope, _swap_halves(rope)], axis=-1)
    return w.reshape(Q_LORA, MLA_HEADS * QK_HEAD).astype(BF16)


def _pool_w_layout(pool_w):
    n = len(POOL_WINDOWS)
    eye = jnp.eye(n, dtype=pool_w.dtype)
    return jnp.einsum('gcd,gh->gchd', pool_w, eye).reshape(POOL_WIDTH, POOL_WIDTH).astype(BF16)


def kernel(x, mem, positions, ln_g, ln_b, ffn1_w13, ffn1_w2, w_in, pool_w, pool_scale, q_norm_g,
           w_uq, kv_norm_g, w_ukv, w_out, mem_wq, mem_wkv, mem_wo, ffn2_w13, ffn2_w2):
    bsz, seq, d = x.shape
    m = bsz * seq
    t_a, t_b = _rope_tables(positions)
    row = lambda v: v[None, :]
    for l in range(DEPTH):
        g = lambda i: row(ln_g[l, i])
        b = lambda i: row(ln_b[l, i])
        x = _ffn_ln(x.reshape(m, d), ffn1_w13[l].astype(BF16), ffn1_w2[l].astype(BF16), g(0), b(0))
        x = x.reshape(bsz, seq, d)
        y_pool, q, k, v = _mix_proj(
            x, t_a, t_b, _w_in_layout(w_in[l]), _pool_w_layout(pool_w[l]), row(pool_scale[l]),
            row(q_norm_g[l]), _w_uq_layout(w_uq[l]), row(kv_norm_g[l]), w_ukv[l].astype(BF16))
        y_mla = _mla_attn(q, k, v)
        x = _out_proj_ln(x.reshape(m, d), y_pool.reshape(m, -1), y_mla.reshape(m, -1),
                         w_out[l].astype(BF16), g(1), b(1))
        k_mem, v_mem = _mem_kv(mem, mem_wkv[l].astype(BF16))
        x = _mem_attn_ln(x.reshape(bsz, seq, d), k_mem, v_mem, mem_wq[l].astype(BF16),
                         mem_wo[l].astype(BF16), g(2), b(2))
        x = _ffn_ln(x.reshape(m, d), ffn2_w13[l].astype(BF16), ffn2_w2[l].astype(BF16), g(3), b(3))
        x = x.reshape(bsz, seq, d)
    return x
```

```python
import functools
import math

import jax
import jax.numpy as jnp
from jax import lax
from jax.experimental import pallas as pl
from jax.experimental.pallas import tpu as pltpu

D_MODEL = 1024
DEPTH = 2
CHUNK = 64
MEM_HEADS = 4
MEM_HEAD_DIM = D_MODEL // MEM_HEADS
POOL_WINDOWS = (2, 4, 8, 16)
POOL_WIDTH = 256
POOL_GROUP = POOL_WIDTH // len(POOL_WINDOWS)
POOL_HALO = 2 * POOL_WINDOWS[-1]
QK_NOPE = 128
QK_ROPE = 64
V_HEAD = 128
MLA_HEADS = 6
QK_HEAD = QK_NOPE + 2 * QK_ROPE
Q_LORA = 256
KV_LORA = 128
ROPE_BASE = 10000.0
D_FF = 2816
ALPHA = (2 * DEPTH) ** 0.25
LN_EPS = 1e-5
RMS_EPS = 1e-6
NEG_INF = -1e30
LOG2_E = math.log2(math.e)

V7X_VMEM_BYTES = 64 * 1024 * 1024
VMEM_LIMIT = 56 * 1024 * 1024

FFN_ROWS = 512
FFN_CHUNK = 1408
ROW_TILE = 512
ATT_BLOCK = 512
ATT_HEADS = 3

BF16 = jnp.bfloat16
F32 = jnp.float32


def _dot(a, b):
    return jnp.dot(a, b, preferred_element_type=F32)


def _dot_nt(a, b):
    return lax.dot_general(a, b, (((1,), (1,)), ((), ())), preferred_element_type=F32)


def _layer_norm(z, g, b):
    mu = jnp.mean(z, axis=-1, keepdims=True)
    zc = z - mu
    var = jnp.mean(zc * zc, axis=-1, keepdims=True)
    return zc * lax.rsqrt(var + LN_EPS) * g + b


def _rms_norm(z, g):
    return z * lax.rsqrt(jnp.mean(z * z, axis=-1, keepdims=True) + RMS_EPS) * g


def _const_spec(shape):
    zeros = (0,) * len(shape)
    return pl.BlockSpec(shape, lambda *_: zeros, pipeline_mode=pl.Buffered(1))


def _params(*semantics):
    return pltpu.CompilerParams(dimension_semantics=semantics, vmem_limit_bytes=VMEM_LIMIT)


def _ffn_ln_kernel(x_ref, w13_ref, w2_ref, g_ref, b_ref, o_ref):
    x = x_ref[...]
    xb = x.astype(BF16)
    acc = None
    for c0 in range(0, D_FF, FFN_CHUNK):
        gate = _dot(xb, w13_ref[:, c0:c0 + FFN_CHUNK])
        up = _dot(xb, w13_ref[:, D_FF + c0:D_FF + c0 + FFN_CHUNK])
        act = (gate * jax.nn.sigmoid(gate) * up).astype(BF16)
        part = _dot(act, w2_ref[c0:c0 + FFN_CHUNK, :])
        acc = part if acc is None else acc + part
    o_ref[...] = _layer_norm(ALPHA * x + 0.5 * acc, g_ref[...], b_ref[...])


def _ffn_ln(x, w13, w2, g, b):
    m = x.shape[0]
    return pl.pallas_call(
        _ffn_ln_kernel,
        grid=(m // FFN_ROWS,),
        in_specs=[
            pl.BlockSpec((FFN_ROWS, D_MODEL), lambda i: (i, 0)),
            _const_spec(w13.shape),
            _const_spec(w2.shape),
            _const_spec(g.shape),
            _const_spec(b.shape),
        ],
        out_specs=pl.BlockSpec((FFN_ROWS, D_MODEL), lambda i: (i, 0)),
        out_shape=jax.ShapeDtypeStruct((m, D_MODEL), F32),
        compiler_params=_params("parallel"),
        name="ffn_ln",
    )(x, w13, w2, g, b)


def _rope_table_kernel(pos_ref, freq_ref, cos_ref, sin_ref):
    ang = pos_ref[...].astype(F32) * freq_ref[...]
    cos_ref[...] = jnp.cos(ang)
    sin_ref[...] = jnp.sin(ang)


def _rope_tables(positions):
    bsz, seq = positions.shape
    half = QK_ROPE // 2
    per_row = 128 // half
    inv_freq = ROPE_BASE ** (-jnp.arange(half, dtype=F32) / half)
    freq_row = jnp.tile(inv_freq, per_row)[None, :]
    pos_rows = jnp.repeat(positions.reshape(-1, per_row), half, axis=1)
    rows = pos_rows.shape[0]
    cos, sin = pl.pallas_call(
        _rope_table_kernel,
        grid=(1,),
        in_specs=[pl.BlockSpec((rows, 128), lambda i: (0, 0)),
                  pl.BlockSpec((1, 128), lambda i: (0, 0))],
        out_specs=[pl.BlockSpec((rows, 128), lambda i: (0, 0))] * 2,
        out_shape=[jax.ShapeDtypeStruct((rows, 128), F32)] * 2,
        compiler_params=_params("arbitrary"),
        name="rope_tables",
    )(pos_rows, freq_row)
    cos = cos.reshape(bsz, seq, half)
    sin = sin.reshape(bsz, seq, half)
    t_a = jnp.concatenate([cos, cos, -sin, sin], axis=-1)
    t_b = jnp.concatenate([-sin, sin, cos, cos], axis=-1)
    return t_a, t_b


def _mix_proj_kernel(x_ref, ta_ref, tb_ref, w_in_ref, pool_w_ref, pool_s_ref, qg_ref, w_uq_ref,
                     kvg_ref, w_uk_ref, w_uvt_ref, ypool_ref, q_ref, k_ref, vt_ref, ext_ref):
    step = pl.program_id(1)
    rows = x_ref.shape[0]

    @pl.when(step == 0)
    def _():
        ext_ref[0:POOL_HALO, :] = jnp.zeros((POOL_HALO, POOL_WIDTH), F32)

    xb = x_ref[...].astype(BF16)
    h = _dot(xb, w_in_ref[...])
    u = h[:, :POOL_WIDTH]
    c_q = h[:, POOL_WIDTH:POOL_WIDTH + Q_LORA]
    o_kv = POOL_WIDTH + Q_LORA
    c_kv = h[:, o_kv:o_kv + KV_LORA]
    kpe_a = h[:, o_kv + KV_LORA:o_kv + KV_LORA + 128]
    kpe_b = h[:, o_kv + KV_LORA + 128:o_kv + KV_LORA + 256]

    half = POOL_HALO // 2
    ext_ref[POOL_HALO:, :] = u
    lane = lax.broadcasted_iota(jnp.int32, (1, POOL_WIDTH), 1)
    win_sum = None
    shift = 1
    while shift < POOL_WINDOWS[-1]:
        cur = ext_ref[half:, :] + ext_ref[pl.ds(half - shift, rows + half), :]
        shift *= 2
        gi = POOL_WINDOWS.index(shift)
        tile_sum = cur[half:, :]
        win_sum = tile_sum if win_sum is None else jnp.where(lane >= gi * POOL_GROUP, tile_sum, win_sum)
        if shift < POOL_WINDOWS[-1]:
            ext_ref[half:, :] = cur
    ext_ref[0:POOL_HALO, :] = u[rows - POOL_HALO:, :]

    t = step * rows + lax.broadcasted_iota(jnp.int32, (rows, 1), 0)
    win = jnp.left_shift(2, lane // POOL_GROUP)
    cnt = jnp.minimum(t + 1, win).astype(F32)
    d = (win_sum / cnt - u).astype(BF16)
    ypool_ref[...] = (_dot(d, pool_w_ref[...]) * pool_s_ref[...]).astype(BF16)

    t_a = ta_ref[...]
    t_b = tb_ref[...]
    scale = (QK_NOPE + QK_ROPE) ** -0.5 * LOG2_E
    qf = _dot(_rms_norm(c_q, qg_ref[...]).astype(BF16), w_uq_ref[...]) * scale
    kvn = _rms_norm(c_kv, kvg_ref[...]).astype(BF16)
    k_nope = _dot(kvn, w_uk_ref[...])
    vt_ref[...] = _dot_nt(w_uvt_ref[...], kvn).astype(BF16)
    k_rot2 = (kpe_a * t_a + kpe_b * t_b).astype(BF16)
    for hd in range(MLA_HEADS):
        o = hd * QK_HEAD
        q_ref[:, o:o + QK_NOPE] = qf[:, o:o + QK_NOPE].astype(BF16)
        q_ref[:, o + QK_NOPE:o + QK_HEAD] = (qf[:, o + QK_NOPE:o + QK_HEAD] * t_a).astype(BF16)
        k_ref[:, o:o + QK_NOPE] = k_nope[:, hd * QK_NOPE:(hd + 1) * QK_NOPE].astype(BF16)
        k_ref[:, o + QK_NOPE:o + QK_HEAD] = k_rot2


def _mix_proj(x, t_a, t_b, w_in, pool_w, pool_s, qg, w_uq, kvg, w_uk, w_uvt):
    bsz, seq, _ = x.shape
    rows = ROW_TILE
    tile = lambda width: pl.BlockSpec((None, rows, width), lambda b, i: (b, i, 0))
    return pl.pallas_call(
        _mix_proj_kernel,
        grid=(bsz, seq // rows),
        in_specs=[tile(D_MODEL), tile(128), tile(128),
                  _const_spec(w_in.shape), _const_spec(pool_w.shape), _const_spec(pool_s.shape),
                  _const_spec(qg.shape), _const_spec(w_uq.shape),
                  _const_spec(kvg.shape), _const_spec(w_uk.shape), _const_spec(w_uvt.shape)],
        out_specs=[tile(POOL_WIDTH), tile(MLA_HEADS * QK_HEAD), tile(MLA_HEADS * QK_HEAD),
                   pl.BlockSpec((None, MLA_HEADS * V_HEAD, rows), lambda b, i: (b, 0, i))],
        out_shape=[jax.ShapeDtypeStruct((bsz, seq, POOL_WIDTH), BF16),
                   jax.ShapeDtypeStruct((bsz, seq, MLA_HEADS * QK_HEAD), BF16),
                   jax.ShapeDtypeStruct((bsz, seq, MLA_HEADS * QK_HEAD), BF16),
                   jax.ShapeDtypeStruct((bsz, MLA_HEADS * V_HEAD, seq), BF16)],
        scratch_shapes=[pltpu.VMEM((rows + POOL_HALO, POOL_WIDTH), F32)],
        compiler_params=_params("parallel", "arbitrary"),
        name="mix_proj",
    )(x, t_a, t_b, w_in, pool_w, pool_s, qg, w_uq, kvg, w_uk, w_uvt)


def _mla_attn_kernel(q_ref, k_ref, vt_ref, o_ref, sa_ref, sb_ref, bmax_a_ref, bmax_b_ref,
                     m_ref, l_ref, acc_ref):
    blk = ATT_BLOCK
    qi = pl.program_id(2)

    def scores(j, s_ref, bmax_ref):
        start = pl.multiple_of(j * blk, blk)
        for hd in range(ATT_HEADS):
            k = k_ref[pl.ds(start, blk), hd * QK_HEAD:(hd + 1) * QK_HEAD]
            s = _dot_nt(k, q_ref[:, hd * QK_HEAD:(hd + 1) * QK_HEAD])
            s_ref[hd] = s
            bmax_ref[hd] = jnp.max(s, axis=0, keepdims=True)

    def update(j, s_ref, bmax_ref, diagonal=False):
        start = pl.multiple_of(j * blk, blk)
        for hd in range(ATT_HEADS):
            s = s_ref[hd]
            bmax = bmax_ref[hd]
            if diagonal:
                kc = lax.broadcasted_iota(jnp.int32, (blk, blk), 0) // CHUNK
                qc = lax.broadcasted_iota(jnp.int32, (blk, blk), 1) // CHUNK
                s = jnp.where(qc >= kc, s, NEG_INF)
                bmax = jnp.max(s, axis=0, keepdims=True)
            m = m_ref[hd]
            m_new = jnp.maximum(m, bmax)
            alpha = jnp.exp2(m - m_new)
            p = jnp.exp2(s - m_new)
            m_ref[hd] = m_new
            l_ref[hd] = alpha * l_ref[hd] + jnp.sum(p, axis=0, keepdims=True)
            vt = vt_ref[hd * V_HEAD:(hd + 1) * V_HEAD, pl.ds(start, blk)]
            acc_ref[hd] = alpha * acc_ref[hd] + _dot(vt, p.astype(BF16))

    m_ref[...] = jnp.full(m_ref.shape, NEG_INF, F32)
    l_ref[...] = jnp.zeros(l_ref.shape, F32)
    acc_ref[...] = jnp.zeros(acc_ref.shape, F32)

    scores(0, sa_ref, bmax_a_ref)

    def pair(i, _):
        j = 2 * i
        scores(j + 1, sb_ref, bmax_b_ref)
        update(j, sa_ref, bmax_a_ref)
        scores(j + 2, sa_ref, bmax_a_ref)
        update(j + 1, sb_ref, bmax_b_ref)
        return 0

    lax.fori_loop(0, qi // 2, pair, 0)

    @pl.when(qi % 2 == 0)
    def _():
        update(qi, sa_ref, bmax_a_ref, diagonal=True)

    @pl.when(qi % 2 == 1)
    def _():
        scores(qi, sb_ref, bmax_b_ref)
        update(qi - 1, sa_ref, bmax_a_ref)
        update(qi, sb_ref, bmax_b_ref, diagonal=True)

    for hd in range(ATT_HEADS):
        o_ref[:, hd * V_HEAD:(hd + 1) * V_HEAD] = (acc_ref[hd] / l_ref[hd]).T.astype(BF16)


def _mla_attn(q, k, vt):
    bsz, seq, _ = q.shape
    blk = ATT_BLOCK
    grp = ATT_HEADS
    return pl.pallas_call(
        _mla_attn_kernel,
        grid=(bsz, MLA_HEADS // grp, seq // blk),
        in_specs=[pl.BlockSpec((None, blk, grp * QK_HEAD), lambda b, h, i: (b, i, h)),
                  pl.BlockSpec((None, seq, grp * QK_HEAD), lambda b, h, i: (b, 0, h)),
                  pl.BlockSpec((None, grp * V_HEAD, seq), lambda b, h, i: (b, h, 0))],
        out_specs=pl.BlockSpec((None, blk, grp * V_HEAD), lambda b, h, i: (b, i, h)),
        out_shape=jax.ShapeDtypeStruct((bsz, seq, MLA_HEADS * V_HEAD), BF16),
        scratch_shapes=[pltpu.VMEM((grp, blk, blk), F32), pltpu.VMEM((grp, blk, blk), F32),
                        pltpu.VMEM((grp, 1, blk), F32), pltpu.VMEM((grp, 1, blk), F32),
                        pltpu.VMEM((grp, 1, blk), F32), pltpu.VMEM((grp, 1, blk), F32),
                        pltpu.VMEM((grp, V_HEAD, blk), F32)],
        compiler_params=_params("parallel", "parallel", "arbitrary"),
        name="mla_attn",
    )(q, k, vt)


def _out_proj_ln_kernel(x_ref, yp_ref, ym_ref, w_ref, g_ref, b_ref, o_ref):
    y = _dot(yp_ref[...], w_ref[0:POOL_WIDTH, :]) + _dot(ym_ref[...], w_ref[POOL_WIDTH:, :])
    o_ref[...] = _layer_norm(ALPHA * x_ref[...] + y, g_ref[...], b_ref[...])


def _out_proj_ln(x, y_pool, y_mla, w_out, g, b):
    m = x.shape[0]
    rows = ROW_TILE
    tile = lambda width: pl.BlockSpec((rows, width), lambda i: (i, 0))
    return pl.pallas_call(
        _out_proj_ln_kernel,
        grid=(m // rows,),
        in_specs=[tile(D_MODEL), tile(POOL_WIDTH), tile(MLA_HEADS * V_HEAD),
                  _const_spec(w_out.shape), _const_spec(g.shape), _const_spec(b.shape)],
        out_specs=tile(D_MODEL),
        out_shape=jax.ShapeDtypeStruct((m, D_MODEL), F32),
        compiler_params=_params("parallel"),
        name="out_proj_ln",
    )(x, y_pool, y_mla, w_out, g, b)


def _mem_kv_kernel(mem_ref, w_ref, k_ref, v_ref):
    kv = _dot(mem_ref[...].astype(BF16), w_ref[...])
    k_ref[...] = kv[:, :D_MODEL].astype(BF16)
    v_ref[...] = kv[:, D_MODEL:].astype(BF16)


def _mem_kv(mem, wkv):
    bsz, n_mem, _ = mem.shape
    blk = lambda: pl.BlockSpec((None, n_mem, D_MODEL), lambda b: (b, 0, 0))
    return pl.pallas_call(
        _mem_kv_kernel,
        grid=(bsz,),
        in_specs=[blk(), _const_spec(wkv.shape)],
        out_specs=[blk(), blk()],
        out_shape=[jax.ShapeDtypeStruct((bsz, n_mem, D_MODEL), BF16)] * 2,
        compiler_params=_params("parallel"),
        name="mem_kv",
    )(mem, wkv)


def _mem_attn_ln_kernel(x_ref, k_ref, v_ref, wq_ref, wo_ref, g_ref, b_ref, o_ref):
    x = x_ref[...]
    q = (_dot(x.astype(BF16), wq_ref[...]) * MEM_HEAD_DIM ** -0.5).astype(BF16)
    heads = []
    for hd in range(MEM_HEADS):
        sl = slice(hd * MEM_HEAD_DIM, (hd + 1) * MEM_HEAD_DIM)
        s = _dot_nt(q[:, sl], k_ref[:, sl])
        p = jnp.exp(s - jnp.max(s, axis=-1, keepdims=True))
        o = _dot(p.astype(BF16), v_ref[:, sl]) / jnp.sum(p, axis=-1, keepdims=True)
        heads.append(o.astype(BF16))
    y = _dot(jnp.concatenate(heads, axis=-1), wo_ref[...])
    o_ref[...] = _layer_norm(ALPHA * x + y, g_ref[...], b_ref[...])


def _mem_attn_ln(x, k_mem, v_mem, wq, wo, g, b):
    bsz, seq, _ = x.shape
    n_mem = k_mem.shape[1]
    rows = ROW_TILE
    tile = pl.BlockSpec((None, rows, D_MODEL), lambda bi, i: (bi, i, 0))
    mem_blk = lambda: pl.BlockSpec((None, n_mem, D_MODEL), lambda bi, i: (bi, 0, 0))
    return pl.pallas_call(
        _mem_attn_ln_kernel,
        grid=(bsz, seq // rows),
        in_specs=[tile, mem_blk(), mem_blk(), _const_spec(wq.shape), _const_spec(wo.shape),
                  _const_spec(g.shape), _const_spec(b.shape)],
        out_specs=tile,
        out_shape=jax.ShapeDtypeStruct((bsz, seq, D_MODEL), F32),
        compiler_params=_params("parallel", "arbitrary"),
        name="mem_attn_ln",
    )(x, k_mem, v_mem, wq, wo, g, b)


def _swap_halves(w):
    half = w.shape[-1] // 2
    return jnp.concatenate([w[..., half:], w[..., :half]], axis=-1)


def _w_in_layout(w_in):
    o_pe = POOL_WIDTH + Q_LORA + KV_LORA
    k_pe = w_in[:, o_pe:]
    k_sw = _swap_halves(k_pe)
    return jnp.concatenate([w_in[:, :o_pe], k_pe, k_sw, k_sw, k_pe], axis=-1).astype(BF16)


def _w_uq_layout(w_uq):
    w = w_uq.reshape(Q_LORA, MLA_HEADS, QK_NOPE + QK_ROPE)
    rope = w[..., QK_NOPE:]
    w = jnp.concatenate([w[..., :QK_NOPE], rope, _swap_halves(rope)], axis=-1)
    return w.reshape(Q_LORA, MLA_HEADS * QK_HEAD).astype(BF16)


def _w_ukv_layout(w_ukv):
    w = w_ukv.reshape(KV_LORA, MLA_HEADS, QK_NOPE + V_HEAD)
    w_uk = w[..., :QK_NOPE].reshape(KV_LORA, MLA_HEADS * QK_NOPE)
    w_uv = w[..., QK_NOPE:].reshape(KV_LORA, MLA_HEADS * V_HEAD)
    return w_uk.astype(BF16), w_uv.T.astype(BF16)


def _pool_w_layout(pool_w):
    n = len(POOL_WINDOWS)
    eye = jnp.eye(n, dtype=pool_w.dtype)
    return jnp.einsum('gcd,gh->gchd', pool_w, eye).reshape(POOL_WIDTH, POOL_WIDTH).astype(BF16)


def kernel(x, mem, positions, ln_g, ln_b, ffn1_w13, ffn1_w2, w_in, pool_w, pool_scale, q_norm_g,
           w_uq, kv_norm_g, w_ukv, w_out, mem_wq, mem_wkv, mem_wo, ffn2_w13, ffn2_w2):
    bsz, seq, d = x.shape
    m = bsz * seq
    t_a, t_b = _rope_tables(positions)
    row = lambda v: v[None, :]
    for l in range(DEPTH):
        g = lambda i: row(ln_g[l, i])
        b = lambda i: row(ln_b[l, i])
        x = _ffn_ln(x.reshape(m, d), ffn1_w13[l].astype(BF16), ffn1_w2[l].astype(BF16), g(0), b(0))
        x = x.reshape(bsz, seq, d)
        w_uk, w_uvt = _w_ukv_layout(w_ukv[l])
        y_pool, q, k, vt = _mix_proj(
            x, t_a, t_b, _w_in_layout(w_in[l]), _pool_w_layout(pool_w[l]), row(pool_scale[l]),
            row(q_norm_g[l]), _w_uq_layout(w_uq[l]), row(kv_norm_g[l]), w_uk, w_uvt)
        y_mla = _mla_attn(q, k, vt)
        x = _out_proj_ln(x.reshape(m, d), y_pool.reshape(m, -1), y_mla.reshape(m, -1),
                         w_out[l].astype(BF16), g(1), b(1))
        k_mem, v_mem = _mem_kv(mem, mem_wkv[l].astype(BF16))
        x = _mem_attn_ln(x.reshape(bsz, seq, d), k_mem, v_mem, mem_wq[l].astype(BF16),
                         mem_wo[l].astype(BF16), g(2), b(2))
        x = _ffn_ln(x.reshape(m, d), ffn2_w13[l].astype(BF16), ffn2_w2[l].astype(BF16), g(3), b(3))
        x = x.reshape(bsz, seq, d)
    return x
```

```python
import functools
import math

import jax
import jax.numpy as jnp
from jax import lax
from jax.experimental import pallas as pl
from jax.experimental.pallas import tpu as pltpu

D_MODEL = 1024
DEPTH = 2
CHUNK = 64
MEM_HEADS = 4
MEM_HEAD_DIM = D_MODEL // MEM_HEADS
POOL_WINDOWS = (2, 4, 8, 16)
POOL_WIDTH = 256
POOL_GROUP = POOL_WIDTH // len(POOL_WINDOWS)
POOL_HALO = 2 * POOL_WINDOWS[-1]
QK_NOPE = 128
QK_ROPE = 64
V_HEAD = 128
MLA_HEADS = 6
QK_HEAD = QK_NOPE + 2 * QK_ROPE
Q_LORA = 256
KV_LORA = 128
ROPE_BASE = 10000.0
D_FF = 2816
ALPHA = (2 * DEPTH) ** 0.25
LN_EPS = 1e-5
RMS_EPS = 1e-6
NEG_INF = -1e30
LOG2_E = math.log2(math.e)

V7X_VMEM_BYTES = 64 * 1024 * 1024
VMEM_LIMIT = 56 * 1024 * 1024

FFN_ROWS = 512
MXU_TILE = 256
FFN_SPLITS = (0, 6 * MXU_TILE, D_FF)
ROW_TILE = 512
ATT_BLOCK = 512
ATT_HEADS = 3

BF16 = jnp.bfloat16
F32 = jnp.float32


def _dot(a, b):
    return jnp.dot(a, b, preferred_element_type=F32)


def _dot_nt(a, b):
    return lax.dot_general(a, b, (((1,), (1,)), ((), ())), preferred_element_type=F32)


def _layer_norm(z, g, b):
    mu = jnp.mean(z, axis=-1, keepdims=True)
    zc = z - mu
    var = jnp.mean(zc * zc, axis=-1, keepdims=True)
    return zc * lax.rsqrt(var + LN_EPS) * g + b


def _rms_norm(z, g):
    return z * lax.rsqrt(jnp.mean(z * z, axis=-1, keepdims=True) + RMS_EPS) * g


def _const_spec(shape):
    zeros = (0,) * len(shape)
    return pl.BlockSpec(shape, lambda *_: zeros, pipeline_mode=pl.Buffered(1))


def _params(*semantics):
    return pltpu.CompilerParams(dimension_semantics=semantics, vmem_limit_bytes=VMEM_LIMIT)


def _swiglu(xb, w13_ref, w2_ref):
    acc = None
    for c0, c1 in zip(FFN_SPLITS[:-1], FFN_SPLITS[1:]):
        gate = _dot(xb, w13_ref[:, c0:c1])
        up = _dot(xb, w13_ref[:, D_FF + c0:D_FF + c1])
        act = (gate * jax.nn.sigmoid(gate) * up).astype(BF16)
        part = _dot(act, w2_ref[c0:c1, :])
        acc = part if acc is None else acc + part
    return acc


def _ffn_ln_kernel(x_ref, w13_ref, w2_ref, g_ref, b_ref, o_ref):
    x = x_ref[...]
    y = _swiglu(x.astype(BF16), w13_ref, w2_ref)
    o_ref[...] = _layer_norm(ALPHA * x + 0.5 * y, g_ref[...], b_ref[...])


def _ffn_ln(x, w13, w2, g, b):
    m = x.shape[0]
    return pl.pallas_call(
        _ffn_ln_kernel,
        grid=(m // FFN_ROWS,),
        in_specs=[
            pl.BlockSpec((FFN_ROWS, D_MODEL), lambda i: (i, 0)),
            _const_spec(w13.shape),
            _const_spec(w2.shape),
            _const_spec(g.shape),
            _const_spec(b.shape),
        ],
        out_specs=pl.BlockSpec((FFN_ROWS, D_MODEL), lambda i: (i, 0)),
        out_shape=jax.ShapeDtypeStruct((m, D_MODEL), F32),
        compiler_params=_params("parallel"),
        name="ffn_ln",
    )(x, w13, w2, g, b)


def _rope_table_kernel(pos_ref, freq_ref, cos_ref, sin_ref):
    ang = pos_ref[...].astype(F32) * freq_ref[...]
    cos_ref[...] = jnp.cos(ang)
    sin_ref[...] = jnp.sin(ang)


def _rope_tables(positions):
    bsz, seq = positions.shape
    half = QK_ROPE // 2
    per_row = 128 // half
    inv_freq = ROPE_BASE ** (-jnp.arange(half, dtype=F32) / half)
    freq_row = jnp.tile(inv_freq, per_row)[None, :]
    pos_rows = jnp.repeat(positions.reshape(-1, per_row), half, axis=1)
    rows = pos_rows.shape[0]
    cos, sin = pl.pallas_call(
        _rope_table_kernel,
        grid=(1,),
        in_specs=[pl.BlockSpec((rows, 128), lambda i: (0, 0)),
                  pl.BlockSpec((1, 128), lambda i: (0, 0))],
        out_specs=[pl.BlockSpec((rows, 128), lambda i: (0, 0))] * 2,
        out_shape=[jax.ShapeDtypeStruct((rows, 128), F32)] * 2,
        compiler_params=_params("arbitrary"),
        name="rope_tables",
    )(pos_rows, freq_row)
    cos = cos.reshape(bsz, seq, half)
    sin = sin.reshape(bsz, seq, half)
    t_a = jnp.concatenate([cos, cos, -sin, sin], axis=-1)
    t_b = jnp.concatenate([-sin, sin, cos, cos], axis=-1)
    return t_a, t_b


def _mix_proj_kernel(x_ref, ta_ref, tb_ref, w_in_ref, pool_w_ref, pool_s_ref, qg_ref, w_uq_ref,
                     kvg_ref, w_uk_ref, w_uvt_ref, ypool_ref, q_ref, k_ref, vt_ref, ext_ref):
    step = pl.program_id(1)
    rows = x_ref.shape[0]

    @pl.when(step == 0)
    def _():
        ext_ref[0:POOL_HALO, :] = jnp.zeros((POOL_HALO, POOL_WIDTH), F32)

    xb = x_ref[...].astype(BF16)
    h = _dot(xb, w_in_ref[...])
    u = h[:, :POOL_WIDTH]
    c_q = h[:, POOL_WIDTH:POOL_WIDTH + Q_LORA]
    o_kv = POOL_WIDTH + Q_LORA
    c_kv = h[:, o_kv:o_kv + KV_LORA]
    kpe_a = h[:, o_kv + KV_LORA:o_kv + KV_LORA + 128]
    kpe_b = h[:, o_kv + KV_LORA + 128:o_kv + KV_LORA + 256]

    half = POOL_HALO // 2
    ext_ref[POOL_HALO:, :] = u
    lane = lax.broadcasted_iota(jnp.int32, (1, POOL_WIDTH), 1)
    win_sum = None
    shift = 1
    while shift < POOL_WINDOWS[-1]:
        cur = ext_ref[half:, :] + ext_ref[pl.ds(half - shift, rows + half), :]
        shift *= 2
        gi = POOL_WINDOWS.index(shift)
        tile_sum = cur[half:, :]
        win_sum = tile_sum if win_sum is None else jnp.where(lane >= gi * POOL_GROUP, tile_sum, win_sum)
        if shift < POOL_WINDOWS[-1]:
            ext_ref[half:, :] = cur
    ext_ref[0:POOL_HALO, :] = u[rows - POOL_HALO:, :]

    t = step * rows + lax.broadcasted_iota(jnp.int32, (rows, 1), 0)
    win = jnp.left_shift(2, lane // POOL_GROUP)
    cnt = jnp.minimum(t + 1, win).astype(F32)
    d = (win_sum / cnt - u).astype(BF16)
    ypool_ref[...] = (_dot(d, pool_w_ref[...]) * pool_s_ref[...]).astype(BF16)

    t_a = ta_ref[...]
    t_b = tb_ref[...]
    scale = (QK_NOPE + QK_ROPE) ** -0.5 * LOG2_E
    qf = _dot(_rms_norm(c_q, qg_ref[...]).astype(BF16), w_uq_ref[...]) * scale
    kvn = _rms_norm(c_kv, kvg_ref[...]).astype(BF16)
    k_nope = _dot(kvn, w_uk_ref[...])
    vt_ref[...] = _dot_nt(w_uvt_ref[...], kvn).astype(BF16)
    k_rot2 = (kpe_a * t_a + kpe_b * t_b).astype(BF16)
    for hd in range(MLA_HEADS):
        o = hd * QK_HEAD
        q_ref[:, o:o + QK_NOPE] = qf[:, o:o + QK_NOPE].astype(BF16)
        q_ref[:, o + QK_NOPE:o + QK_HEAD] = (qf[:, o + QK_NOPE:o + QK_HEAD] * t_a).astype(BF16)
        k_ref[:, o:o + QK_NOPE] = k_nope[:, hd * QK_NOPE:(hd + 1) * QK_NOPE].astype(BF16)
        k_ref[:, o + QK_NOPE:o + QK_HEAD] = k_rot2


def _mix_proj(x, t_a, t_b, w_in, pool_w, pool_s, qg, w_uq, kvg, w_uk, w_uvt):
    bsz, seq, _ = x.shape
    rows = ROW_TILE
    tile = lambda width: pl.BlockSpec((None, rows, width), lambda b, i: (b, i, 0))
    return pl.pallas_call(
        _mix_proj_kernel,
        grid=(bsz, seq // rows),
        in_specs=[tile(D_MODEL), tile(128), tile(128),
                  _const_spec(w_in.shape), _const_spec(pool_w.shape), _const_spec(pool_s.shape),
                  _const_spec(qg.shape), _const_spec(w_uq.shape),
                  _const_spec(kvg.shape), _const_spec(w_uk.shape), _const_spec(w_uvt.shape)],
        out_specs=[tile(POOL_WIDTH), tile(MLA_HEADS * QK_HEAD), tile(MLA_HEADS * QK_HEAD),
                   pl.BlockSpec((None, MLA_HEADS * V_HEAD, rows), lambda b, i: (b, 0, i))],
        out_shape=[jax.ShapeDtypeStruct((bsz, seq, POOL_WIDTH), BF16),
                   jax.ShapeDtypeStruct((bsz, seq, MLA_HEADS * QK_HEAD), BF16),
                   jax.ShapeDtypeStruct((bsz, seq, MLA_HEADS * QK_HEAD), BF16),
                   jax.ShapeDtypeStruct((bsz, MLA_HEADS * V_HEAD, seq), BF16)],
        scratch_shapes=[pltpu.VMEM((rows + POOL_HALO, POOL_WIDTH), F32)],
        compiler_params=_params("parallel", "arbitrary"),
        name="mix_proj",
    )(x, t_a, t_b, w_in, pool_w, pool_s, qg, w_uq, kvg, w_uk, w_uvt)


def _mla_attn_kernel(q_ref, k_ref, vt_ref, o_ref, sa_ref, sb_ref, bmax_a_ref, bmax_b_ref,
                     m_ref, l_ref, acc_ref):
    blk = ATT_BLOCK
    qi = pl.program_id(2)

    def scores(j, s_ref, bmax_ref):
        start = pl.multiple_of(j * blk, blk)
        for hd in range(ATT_HEADS):
            k = k_ref[pl.ds(start, blk), hd * QK_HEAD:(hd + 1) * QK_HEAD]
            s = _dot_nt(k, q_ref[:, hd * QK_HEAD:(hd + 1) * QK_HEAD])
            s_ref[hd] = s
            bmax_ref[hd] = jnp.max(s, axis=0, keepdims=True)

    def update(j, s_ref, bmax_ref, diagonal=False):
        start = pl.multiple_of(j * blk, blk)
        for hd in range(ATT_HEADS):
            s = s_ref[hd]
            bmax = bmax_ref[hd]
            if diagonal:
                kc = lax.broadcasted_iota(jnp.int32, (blk, blk), 0) // CHUNK
                qc = lax.broadcasted_iota(jnp.int32, (blk, blk), 1) // CHUNK
                s = jnp.where(qc >= kc, s, NEG_INF)
                bmax = jnp.max(s, axis=0, keepdims=True)
            m = m_ref[hd]
            m_new = jnp.maximum(m, bmax)
            alpha = jnp.exp2(m - m_new)
            p = jnp.exp2(s - m_new)
            m_ref[hd] = m_new
            l_ref[hd] = alpha * l_ref[hd] + jnp.sum(p, axis=0, keepdims=True)
            vt = vt_ref[hd * V_HEAD:(hd + 1) * V_HEAD, pl.ds(start, blk)]
            acc_ref[hd] = alpha * acc_ref[hd] + _dot(vt, p.astype(BF16))

    m_ref[...] = jnp.full(m_ref.shape, NEG_INF, F32)
    l_ref[...] = jnp.zeros(l_ref.shape, F32)
    acc_ref[...] = jnp.zeros(acc_ref.shape, F32)

    scores(0, sa_ref, bmax_a_ref)

    def pair(i, _):
        j = 2 * i
        scores(j + 1, sb_ref, bmax_b_ref)
        update(j, sa_ref, bmax_a_ref)
        scores(j + 2, sa_ref, bmax_a_ref)
        update(j + 1, sb_ref, bmax_b_ref)
        return 0

    lax.fori_loop(0, qi // 2, pair, 0)

    @pl.when(qi % 2 == 0)
    def _():
        update(qi, sa_ref, bmax_a_ref, diagonal=True)

    @pl.when(qi % 2 == 1)
    def _():
        scores(qi, sb_ref, bmax_b_ref)
        update(qi - 1, sa_ref, bmax_a_ref)
        update(qi, sb_ref, bmax_b_ref, diagonal=True)

    for hd in range(ATT_HEADS):
        o_ref[:, hd * V_HEAD:(hd + 1) * V_HEAD] = (acc_ref[hd] / l_ref[hd]).T.astype(BF16)


def _mla_attn(q, k, vt):
    bsz, seq, _ = q.shape
    blk = ATT_BLOCK
    grp = ATT_HEADS
    return pl.pallas_call(
        _mla_attn_kernel,
        grid=(bsz, MLA_HEADS // grp, seq // blk),
        in_specs=[pl.BlockSpec((None, blk, grp * QK_HEAD), lambda b, h, i: (b, i, h)),
                  pl.BlockSpec((None, seq, grp * QK_HEAD), lambda b, h, i: (b, 0, h)),
                  pl.BlockSpec((None, grp * V_HEAD, seq), lambda b, h, i: (b, h, 0))],
        out_specs=pl.BlockSpec((None, blk, grp * V_HEAD), lambda b, h, i: (b, i, h)),
        out_shape=jax.ShapeDtypeStruct((bsz, seq, MLA_HEADS * V_HEAD), BF16),
        scratch_shapes=[pltpu.VMEM((grp, blk, blk), F32), pltpu.VMEM((grp, blk, blk), F32),
                        pltpu.VMEM((grp, 1, blk), F32), pltpu.VMEM((grp, 1, blk), F32),
                        pltpu.VMEM((grp, 1, blk), F32), pltpu.VMEM((grp, 1, blk), F32),
                        pltpu.VMEM((grp, V_HEAD, blk), F32)],
        compiler_params=_params("parallel", "parallel", "arbitrary"),
        name="mla_attn",
    )(q, k, vt)


def _mem_kv_kernel(mem_ref, w_ref, k_ref, v_ref):
    kv = _dot(mem_ref[...].astype(BF16), w_ref[...])
    k_ref[...] = kv[:, :D_MODEL].astype(BF16)
    v_ref[...] = kv[:, D_MODEL:].astype(BF16)


def _mem_kv(mem, wkv):
    bsz, n_mem, _ = mem.shape
    blk = lambda: pl.BlockSpec((None, n_mem, D_MODEL), lambda b: (b, 0, 0))
    return pl.pallas_call(
        _mem_kv_kernel,
        grid=(bsz,),
        in_specs=[blk(), _const_spec(wkv.shape)],
        out_specs=[blk(), blk()],
        out_shape=[jax.ShapeDtypeStruct((bsz, n_mem, D_MODEL), BF16)] * 2,
        compiler_params=_params("parallel"),
        name="mem_kv",
    )(mem, wkv)


def _mem_attention(x, k_ref, v_ref, wq_ref, wo_ref):
    q = (_dot(x.astype(BF16), wq_ref[...]) * (MEM_HEAD_DIM ** -0.5 * LOG2_E)).astype(BF16)
    heads = []
    for hd in range(MEM_HEADS):
        sl = slice(hd * MEM_HEAD_DIM, (hd + 1) * MEM_HEAD_DIM)
        s = _dot_nt(q[:, sl], k_ref[:, sl])
        p = jnp.exp2(s - jnp.max(s, axis=-1, keepdims=True))
        o = _dot(p.astype(BF16), v_ref[:, sl]) / jnp.sum(p, axis=-1, keepdims=True)
        heads.append(o.astype(BF16))
    return _dot(jnp.concatenate(heads, axis=-1), wo_ref[...])


def _post_attn_kernel(x_ref, yp_ref, ym_ref, k_ref, v_ref, w_out_ref, wq_ref, wo_ref, w13_ref,
                      w2_ref, g_ref, b_ref, o_ref):
    ln = lambda z, i: _layer_norm(z, g_ref[i:i + 1, :], b_ref[i:i + 1, :])
    y = _dot(yp_ref[...], w_out_ref[0:POOL_WIDTH, :]) + _dot(ym_ref[...], w_out_ref[POOL_WIDTH:, :])
    x = ln(ALPHA * x_ref[...] + y, 1)
    x = ln(ALPHA * x + _mem_attention(x, k_ref, v_ref, wq_ref, wo_ref), 2)
    x = ln(ALPHA * x + 0.5 * _swiglu(x.astype(BF16), w13_ref, w2_ref), 3)
    o_ref[...] = x


def _post_attn(x, y_pool, y_mla, k_mem, v_mem, w_out, wq, wo, w13, w2, g, b):
    bsz, seq, _ = x.shape
    n_mem = k_mem.shape[1]
    rows = ROW_TILE
    tile = lambda width: pl.BlockSpec((None, rows, width), lambda bi, i: (bi, i, 0))
    mem_blk = lambda: pl.BlockSpec((None, n_mem, D_MODEL), lambda bi, i: (bi, 0, 0))
    weights = (w_out, wq, wo, w13, w2, g, b)
    return pl.pallas_call(
        _post_attn_kernel,
        grid=(bsz, seq // rows),
        in_specs=[tile(D_MODEL), tile(POOL_WIDTH), tile(MLA_HEADS * V_HEAD), mem_blk(), mem_blk()]
        + [_const_spec(w.shape) for w in weights],
        out_specs=tile(D_MODEL),
        out_shape=jax.ShapeDtypeStruct((bsz, seq, D_MODEL), F32),
        compiler_params=_params("parallel", "arbitrary"),
        name="post_attn",
    )(x, y_pool, y_mla, k_mem, v_mem, *weights)


def _swap_halves(w):
    half = w.shape[-1] // 2
    return jnp.concatenate([w[..., half:], w[..., :half]], axis=-1)


def _w_in_layout(w_in):
    o_pe = POOL_WIDTH + Q_LORA + KV_LORA
    k_pe = w_in[:, o_pe:]
    k_sw = _swap_halves(k_pe)
    return jnp.concatenate([w_in[:, :o_pe], k_pe, k_sw, k_sw, k_pe], axis=-1).astype(BF16)


def _w_uq_layout(w_uq):
    w = w_uq.reshape(Q_LORA, MLA_HEADS, QK_NOPE + QK_ROPE)
    rope = w[..., QK_NOPE:]
    w = jnp.concatenate([w[..., :QK_NOPE], rope, _swap_halves(rope)], axis=-1)
    return w.reshape(Q_LORA, MLA_HEADS * QK_HEAD).astype(BF16)


def _w_ukv_layout(w_ukv):
    w = w_ukv.reshape(KV_LORA, MLA_HEADS, QK_NOPE + V_HEAD)
    w_uk = w[..., :QK_NOPE].reshape(KV_LORA, MLA_HEADS * QK_NOPE)
    w_uv = w[..., QK_NOPE:].reshape(KV_LORA, MLA_HEADS * V_HEAD)
    return w_uk.astype(BF16), w_uv.T.astype(BF16)


def _pool_w_layout(pool_w):
    n = len(POOL_WINDOWS)
    eye = jnp.eye(n, dtype=pool_w.dtype)
    return jnp.einsum('gcd,gh->gchd', pool_w, eye).reshape(POOL_WIDTH, POOL_WIDTH).astype(BF16)


def kernel(x, mem, positions, ln_g, ln_b, ffn1_w13, ffn1_w2, w_in, pool_w, pool_scale, q_norm_g,
           w_uq, kv_norm_g, w_ukv, w_out, mem_wq, mem_wkv, mem_wo, ffn2_w13, ffn2_w2):
    bsz, seq, d = x.shape
    m = bsz * seq
    t_a, t_b = _rope_tables(positions)
    row = lambda v: v[None, :]
    for l in range(DEPTH):
        g = lambda i: row(ln_g[l, i])
        b = lambda i: row(ln_b[l, i])
        x = _ffn_ln(x.reshape(m, d), ffn1_w13[l].astype(BF16), ffn1_w2[l].astype(BF16), g(0), b(0))
        x = x.reshape(bsz, seq, d)
        w_uk, w_uvt = _w_ukv_layout(w_ukv[l])
        y_pool, q, k, vt = _mix_proj(
            x, t_a, t_b, _w_in_layout(w_in[l]), _pool_w_layout(pool_w[l]), row(pool_scale[l]),
            row(q_norm_g[l]), _w_uq_layout(w_uq[l]), row(kv_norm_g[l]), w_uk, w_uvt)
        y_mla = _mla_attn(q, k, vt)
        k_mem, v_mem = _mem_kv(mem, mem_wkv[l].astype(BF16))
        x = _post_attn(x, y_pool, y_mla, k_mem, v_mem, w_out[l].astype(BF16),
                       mem_wq[l].astype(BF16), mem_wo[l].astype(BF16), ffn2_w13[l].astype(BF16),
                       ffn2_w2[l].astype(BF16), ln_g[l], ln_b[l])
    return x
```

```python
import functools
import math

import jax
import jax.numpy as jnp
from jax import lax
from jax.experimental import pallas as pl
from jax.experimental.pallas import tpu as pltpu

D_MODEL = 1024
DEPTH = 2
CHUNK = 64
MEM_HEADS = 4
MEM_HEAD_DIM = D_MODEL // MEM_HEADS
POOL_WINDOWS = (2, 4, 8, 16)
POOL_WIDTH = 256
POOL_GROUP = POOL_WIDTH // len(POOL_WINDOWS)
POOL_HALO = 2 * POOL_WINDOWS[-1]
QK_NOPE = 128
QK_ROPE = 64
V_HEAD = 128
MLA_HEADS = 6
QK_HEAD = QK_NOPE + 2 * QK_ROPE
Q_LORA = 256
KV_LORA = 128
ROPE_BASE = 10000.0
D_FF = 2816
ALPHA = (2 * DEPTH) ** 0.25
LN_EPS = 1e-5
RMS_EPS = 1e-6
NEG_INF = -1e30
LOG2_E = math.log2(math.e)

V7X_VMEM_BYTES = 64 * 1024 * 1024
VMEM_LIMIT = 56 * 1024 * 1024

MXU_TILE = 256
FFN_SPLITS = (0, 6 * MXU_TILE, D_FF)
ROW_TILE = 512
ATT_BLOCK = 512
ATT_HEADS = 3

BF16 = jnp.bfloat16
F32 = jnp.float32


def _dot(a, b):
    return jnp.dot(a, b, preferred_element_type=F32)


def _dot_nt(a, b):
    return lax.dot_general(a, b, (((1,), (1,)), ((), ())), preferred_element_type=F32)


def _layer_norm(z, g, b):
    mu = jnp.mean(z, axis=-1, keepdims=True)
    zc = z - mu
    var = jnp.mean(zc * zc, axis=-1, keepdims=True)
    return zc * lax.rsqrt(var + LN_EPS) * g + b


def _rms_norm(z, g):
    return z * lax.rsqrt(jnp.mean(z * z, axis=-1, keepdims=True) + RMS_EPS) * g


def _const_spec(shape):
    zeros = (0,) * len(shape)
    return pl.BlockSpec(shape, lambda *_: zeros, pipeline_mode=pl.Buffered(1))


def _params(*semantics):
    return pltpu.CompilerParams(dimension_semantics=semantics, vmem_limit_bytes=VMEM_LIMIT)


def _swiglu(xb, w13_ref, w2_ref):
    acc = None
    for c0, c1 in zip(FFN_SPLITS[:-1], FFN_SPLITS[1:]):
        gate = _dot(xb, w13_ref[:, c0:c1])
        up = _dot(xb, w13_ref[:, D_FF + c0:D_FF + c1])
        act = (gate * jax.nn.sigmoid(gate) * up).astype(BF16)
        part = _dot(act, w2_ref[c0:c1, :])
        acc = part if acc is None else acc + part
    return acc


def _rope_table_kernel(pos_ref, freq_ref, cos_ref, sin_ref):
    ang = pos_ref[...].astype(F32) * freq_ref[...]
    cos_ref[...] = jnp.cos(ang)
    sin_ref[...] = jnp.sin(ang)


def _rope_tables(positions):
    bsz, seq = positions.shape
    half = QK_ROPE // 2
    per_row = 128 // half
    inv_freq = ROPE_BASE ** (-jnp.arange(half, dtype=F32) / half)
    freq_row = jnp.tile(inv_freq, per_row)[None, :]
    pos_rows = jnp.repeat(positions.reshape(-1, per_row), half, axis=1)
    rows = pos_rows.shape[0]
    cos, sin = pl.pallas_call(
        _rope_table_kernel,
        grid=(1,),
        in_specs=[pl.BlockSpec((rows, 128), lambda i: (0, 0)),
                  pl.BlockSpec((1, 128), lambda i: (0, 0))],
        out_specs=[pl.BlockSpec((rows, 128), lambda i: (0, 0))] * 2,
        out_shape=[jax.ShapeDtypeStruct((rows, 128), F32)] * 2,
        compiler_params=_params("arbitrary"),
        name="rope_tables",
    )(pos_rows, freq_row)
    cos = cos.reshape(bsz, seq, half)
    sin = sin.reshape(bsz, seq, half)
    t_a = jnp.concatenate([cos, cos, -sin, sin], axis=-1)
    t_b = jnp.concatenate([-sin, sin, cos, cos], axis=-1)
    return t_a, t_b


def _pre_attn_kernel(x_ref, ta_ref, tb_ref, w13_ref, w2_ref, g_ref, b_ref, w_in_ref, pool_w_ref,
                     pool_s_ref, qg_ref, w_uq_ref, kvg_ref, w_uk_ref, w_uvt_ref,
                     xo_ref, ypool_ref, q_ref, k_ref, vt_ref, ext_ref):
    step = pl.program_id(1)
    rows = x_ref.shape[0]

    @pl.when(step == 0)
    def _():
        ext_ref[0:POOL_HALO, :] = jnp.zeros((POOL_HALO, POOL_WIDTH), F32)

    x = x_ref[...]
    x = _layer_norm(ALPHA * x + 0.5 * _swiglu(x.astype(BF16), w13_ref, w2_ref),
                    g_ref[0:1, :], b_ref[0:1, :])
    xo_ref[...] = x
    xb = x.astype(BF16)
    h = _dot(xb, w_in_ref[...])
    u = h[:, :POOL_WIDTH]
    c_q = h[:, POOL_WIDTH:POOL_WIDTH + Q_LORA]
    o_kv = POOL_WIDTH + Q_LORA
    c_kv = h[:, o_kv:o_kv + KV_LORA]
    kpe_a = h[:, o_kv + KV_LORA:o_kv + KV_LORA + 128]
    kpe_b = h[:, o_kv + KV_LORA + 128:o_kv + KV_LORA + 256]

    half = POOL_HALO // 2
    ext_ref[POOL_HALO:, :] = u
    lane = lax.broadcasted_iota(jnp.int32, (1, POOL_WIDTH), 1)
    win_sum = None
    shift = 1
    while shift < POOL_WINDOWS[-1]:
        cur = ext_ref[half:, :] + ext_ref[pl.ds(half - shift, rows + half), :]
        shift *= 2
        gi = POOL_WINDOWS.index(shift)
        tile_sum = cur[half:, :]
        win_sum = tile_sum if win_sum is None else jnp.where(lane >= gi * POOL_GROUP, tile_sum, win_sum)
        if shift < POOL_WINDOWS[-1]:
            ext_ref[half:, :] = cur
    ext_ref[0:POOL_HALO, :] = u[rows - POOL_HALO:, :]

    t = step * rows + lax.broadcasted_iota(jnp.int32, (rows, 1), 0)
    win = jnp.left_shift(2, lane // POOL_GROUP)
    cnt = jnp.minimum(t + 1, win).astype(F32)
    d = (win_sum / cnt - u).astype(BF16)
    ypool_ref[...] = (_dot(d, pool_w_ref[...]) * pool_s_ref[...]).astype(BF16)

    t_a = ta_ref[...]
    t_b = tb_ref[...]
    scale = (QK_NOPE + QK_ROPE) ** -0.5 * LOG2_E
    qf = _dot(_rms_norm(c_q, qg_ref[...]).astype(BF16), w_uq_ref[...]) * scale
    kvn = _rms_norm(c_kv, kvg_ref[...]).astype(BF16)
    k_nope = _dot(kvn, w_uk_ref[...])
    vt_ref[...] = _dot_nt(w_uvt_ref[...], kvn).astype(BF16)
    k_rot2 = (kpe_a * t_a + kpe_b * t_b).astype(BF16)
    for hd in range(MLA_HEADS):
        o = hd * QK_HEAD
        q_ref[:, o:o + QK_NOPE] = qf[:, o:o + QK_NOPE].astype(BF16)
        q_ref[:, o + QK_NOPE:o + QK_HEAD] = (qf[:, o + QK_NOPE:o + QK_HEAD] * t_a).astype(BF16)
        k_ref[:, o:o + QK_NOPE] = k_nope[:, hd * QK_NOPE:(hd + 1) * QK_NOPE].astype(BF16)
        k_ref[:, o + QK_NOPE:o + QK_HEAD] = k_rot2


def _pre_attn(x, t_a, t_b, *weights):
    bsz, seq, _ = x.shape
    rows = ROW_TILE
    tile = lambda width: pl.BlockSpec((None, rows, width), lambda b, i: (b, i, 0))
    return pl.pallas_call(
        _pre_attn_kernel,
        grid=(bsz, seq // rows),
        in_specs=[tile(D_MODEL), tile(128), tile(128)] + [_const_spec(w.shape) for w in weights],
        out_specs=[tile(D_MODEL), tile(POOL_WIDTH), tile(MLA_HEADS * QK_HEAD),
                   tile(MLA_HEADS * QK_HEAD),
                   pl.BlockSpec((None, MLA_HEADS * V_HEAD, rows), lambda b, i: (b, 0, i))],
        out_shape=[jax.ShapeDtypeStruct((bsz, seq, D_MODEL), F32),
                   jax.ShapeDtypeStruct((bsz, seq, POOL_WIDTH), BF16),
                   jax.ShapeDtypeStruct((bsz, seq, MLA_HEADS * QK_HEAD), BF16),
                   jax.ShapeDtypeStruct((bsz, seq, MLA_HEADS * QK_HEAD), BF16),
                   jax.ShapeDtypeStruct((bsz, MLA_HEADS * V_HEAD, seq), BF16)],
        scratch_shapes=[pltpu.VMEM((rows + POOL_HALO, POOL_WIDTH), F32)],
        compiler_params=_params("parallel", "arbitrary"),
        name="pre_attn",
    )(x, t_a, t_b, *weights)


def _mla_attn_kernel(q_lo_ref, q_hi_ref, k_ref, vt_ref, o_lo_ref, o_hi_ref, q2_ref, sa_ref, sb_ref,
                     bmax_a_ref, bmax_b_ref, m_ref, l_ref, acc_ref, *, n_blk):
    blk = ATT_BLOCK
    n_tasks = n_blk + 1
    i = pl.program_id(2)
    n_hi = n_blk - 1 - i

    def task(t):
        if isinstance(t, int) and t == 0:
            return 1, n_hi, True
        if isinstance(t, int) and t == n_tasks - 1:
            return 0, i, True
        u = t - 1
        is_hi = u < n_hi
        return jnp.where(is_hi, 1, 0), jnp.where(is_hi, u, u - n_hi), False

    def scores(t, s_ref, bmax_ref):
        sel, j, diagonal = task(t)
        start = pl.multiple_of(j * blk, blk)
        for hd in range(ATT_HEADS):
            cols = slice(hd * QK_HEAD, (hd + 1) * QK_HEAD)
            s = _dot_nt(k_ref[pl.ds(start, blk), cols], q2_ref[sel, :, cols])
            if diagonal:
                kc = lax.broadcasted_iota(jnp.int32, (blk, blk), 0) // CHUNK
                qc = lax.broadcasted_iota(jnp.int32, (blk, blk), 1) // CHUNK
                s = jnp.where(qc >= kc, s, NEG_INF)
            s_ref[hd] = s
            bmax_ref[hd] = jnp.max(s, axis=0, keepdims=True)

    def update(t, s_ref, bmax_ref):
        sel, j, _ = task(t)
        start = pl.multiple_of(j * blk, blk)
        for hd in range(ATT_HEADS):
            m = m_ref[sel, hd]
            m_new = jnp.maximum(m, bmax_ref[hd])
            alpha = jnp.exp2(m - m_new)
            p = jnp.exp2(s_ref[hd] - m_new)
            m_ref[sel, hd] = m_new
            l_ref[sel, hd] = alpha * l_ref[sel, hd] + jnp.sum(p, axis=0, keepdims=True)
            vt = vt_ref[hd * V_HEAD:(hd + 1) * V_HEAD, pl.ds(start, blk)]
            acc_ref[sel, hd] = alpha * acc_ref[sel, hd] + _dot(vt, p.astype(BF16))

    q2_ref[0] = q_lo_ref[...]
    q2_ref[1] = q_hi_ref[...]
    m_ref[...] = jnp.full(m_ref.shape, NEG_INF, F32)
    l_ref[...] = jnp.zeros(l_ref.shape, F32)
    acc_ref[...] = jnp.zeros(acc_ref.shape, F32)

    buf_a = (sa_ref, bmax_a_ref)
    buf_b = (sb_ref, bmax_b_ref)
    scores(0, *buf_a)
    scores(1, *buf_b)
    update(0, *buf_a)

    def pair(p, _):
        t = 1 + 2 * p
        scores(t + 1, *buf_a)
        update(t, *buf_b)
        scores(t + 2, *buf_b)
        update(t + 1, *buf_a)
        return 0

    lax.fori_loop(0, (n_tasks - 3) // 2, pair, 0)
    scores(n_tasks - 1, *buf_a)
    update(n_tasks - 2, *buf_b)
    update(n_tasks - 1, *buf_a)

    for sel, o_ref in enumerate((o_lo_ref, o_hi_ref)):
        for hd in range(ATT_HEADS):
            out = (acc_ref[sel, hd] / l_ref[sel, hd]).T
            o_ref[:, hd * V_HEAD:(hd + 1) * V_HEAD] = out.astype(BF16)


def _mla_attn(q, k, vt):
    bsz, seq, _ = q.shape
    blk = ATT_BLOCK
    grp = ATT_HEADS
    n_blk = seq // blk
    assert n_blk % 2 == 0
    q_spec = lambda hi: pl.BlockSpec((None, blk, grp * QK_HEAD),
                                     lambda b, h, i: (b, n_blk - 1 - i if hi else i, h))
    o_spec = lambda hi: pl.BlockSpec((None, blk, grp * V_HEAD),
                                     lambda b, h, i: (b, n_blk - 1 - i if hi else i, h))
    out = jax.ShapeDtypeStruct((bsz, seq, MLA_HEADS * V_HEAD), BF16)
    return pl.pallas_call(
        functools.partial(_mla_attn_kernel, n_blk=n_blk),
        grid=(bsz, MLA_HEADS // grp, n_blk // 2),
        in_specs=[q_spec(False), q_spec(True),
                  pl.BlockSpec((None, seq, grp * QK_HEAD), lambda b, h, i: (b, 0, h)),
                  pl.BlockSpec((None, grp * V_HEAD, seq), lambda b, h, i: (b, h, 0))],
        out_specs=[o_spec(False), o_spec(True)],
        out_shape=[out, out],
        scratch_shapes=[pltpu.VMEM((2, blk, grp * QK_HEAD), BF16),
                        pltpu.VMEM((grp, blk, blk), F32), pltpu.VMEM((grp, blk, blk), F32),
                        pltpu.VMEM((grp, 1, blk), F32), pltpu.VMEM((grp, 1, blk), F32),
                        pltpu.VMEM((2, grp, 1, blk), F32), pltpu.VMEM((2, grp, 1, blk), F32),
                        pltpu.VMEM((2, grp, V_HEAD, blk), F32)],
        compiler_params=_params("parallel", "parallel", "arbitrary"),
        name="mla_attn",
    )(q, q, k, vt)


def _mem_kv_kernel(mem_ref, w_ref, k_ref, v_ref):
    kv = _dot(mem_ref[...].astype(BF16), w_ref[...])
    k_ref[...] = kv[:, :D_MODEL].astype(BF16)
    v_ref[...] = kv[:, D_MODEL:].astype(BF16)


def _mem_kv(mem, wkv):
    bsz, n_mem, _ = mem.shape
    blk = lambda: pl.BlockSpec((None, n_mem, D_MODEL), lambda b: (b, 0, 0))
    return pl.pallas_call(
        _mem_kv_kernel,
        grid=(bsz,),
        in_specs=[blk(), _const_spec(wkv.shape)],
        out_specs=[blk(), blk()],
        out_shape=[jax.ShapeDtypeStruct((bsz, n_mem, D_MODEL), BF16)] * 2,
        compiler_params=_params("parallel"),
        name="mem_kv",
    )(mem, wkv)


def _mem_attention(x, k_ref, v_ref, wq_ref, wo_ref):
    q = (_dot(x.astype(BF16), wq_ref[...]) * (MEM_HEAD_DIM ** -0.5 * LOG2_E)).astype(BF16)
    heads = []
    for hd in range(MEM_HEADS):
        sl = slice(hd * MEM_HEAD_DIM, (hd + 1) * MEM_HEAD_DIM)
        s = _dot_nt(q[:, sl], k_ref[:, sl])
        p = jnp.exp2(s - jnp.max(s, axis=-1, keepdims=True))
        o = _dot(p.astype(BF16), v_ref[:, sl]) / jnp.sum(p, axis=-1, keepdims=True)
        heads.append(o.astype(BF16))
    return _dot(jnp.concatenate(heads, axis=-1), wo_ref[...])


def _post_attn_kernel(x_ref, yp_ref, ym_lo_ref, ym_hi_ref, k_ref, v_ref, w_out_ref, wq_ref, wo_ref,
                      w13_ref, w2_ref, g_ref, b_ref, o_ref):
    ln = lambda z, i: _layer_norm(z, g_ref[i:i + 1, :], b_ref[i:i + 1, :])
    first_half = pl.program_id(1) < pl.num_programs(1) // 2
    ym = jnp.where(first_half, ym_lo_ref[...], ym_hi_ref[...])
    y = _dot(yp_ref[...], w_out_ref[0:POOL_WIDTH, :]) + _dot(ym, w_out_ref[POOL_WIDTH:, :])
    x = ln(ALPHA * x_ref[...] + y, 1)
    x = ln(ALPHA * x + _mem_attention(x, k_ref, v_ref, wq_ref, wo_ref), 2)
    x = ln(ALPHA * x + 0.5 * _swiglu(x.astype(BF16), w13_ref, w2_ref), 3)
    o_ref[...] = x


def _post_attn(x, y_pool, y_mla_lo, y_mla_hi, k_mem, v_mem, w_out, wq, wo, w13, w2, g, b):
    bsz, seq, _ = x.shape
    n_mem = k_mem.shape[1]
    rows = ATT_BLOCK
    half = seq // rows // 2
    tile = lambda width: pl.BlockSpec((None, rows, width), lambda bi, i: (bi, i, 0))
    y_lo = pl.BlockSpec((None, rows, MLA_HEADS * V_HEAD), lambda bi, i: (bi, jnp.minimum(i, half - 1), 0))
    y_hi = pl.BlockSpec((None, rows, MLA_HEADS * V_HEAD), lambda bi, i: (bi, jnp.maximum(i, half), 0))
    mem_blk = lambda: pl.BlockSpec((None, n_mem, D_MODEL), lambda bi, i: (bi, 0, 0))
    weights = (w_out, wq, wo, w13, w2, g, b)
    return pl.pallas_call(
        _post_attn_kernel,
        grid=(bsz, seq // rows),
        in_specs=[tile(D_MODEL), tile(POOL_WIDTH), y_lo, y_hi, mem_blk(), mem_blk()]
        + [_const_spec(w.shape) for w in weights],
        out_specs=tile(D_MODEL),
        out_shape=jax.ShapeDtypeStruct((bsz, seq, D_MODEL), F32),
        compiler_params=_params("parallel", "arbitrary"),
        name="post_attn",
    )(x, y_pool, y_mla_lo, y_mla_hi, k_mem, v_mem, *weights)


def _swap_halves(w):
    half = w.shape[-1] // 2
    return jnp.concatenate([w[..., half:], w[..., :half]], axis=-1)


def _w_in_layout(w_in):
    o_pe = POOL_WIDTH + Q_LORA + KV_LORA
    k_pe = w_in[:, o_pe:]
    k_sw = _swap_halves(k_pe)
    return jnp.concatenate([w_in[:, :o_pe], k_pe, k_sw, k_sw, k_pe], axis=-1).astype(BF16)


def _w_uq_layout(w_uq):
    w = w_uq.reshape(Q_LORA, MLA_HEADS, QK_NOPE + QK_ROPE)
    rope = w[..., QK_NOPE:]
    w = jnp.concatenate([w[..., :QK_NOPE], rope, _swap_halves(rope)], axis=-1)
    return w.reshape(Q_LORA, MLA_HEADS * QK_HEAD).astype(BF16)


def _w_ukv_layout(w_ukv):
    w = w_ukv.reshape(KV_LORA, MLA_HEADS, QK_NOPE + V_HEAD)
    w_uk = w[..., :QK_NOPE].reshape(KV_LORA, MLA_HEADS * QK_NOPE)
    w_uv = w[..., QK_NOPE:].reshape(KV_LORA, MLA_HEADS * V_HEAD)
    return w_uk.astype(BF16), w_uv.T.astype(BF16)


def _pool_w_layout(pool_w):
    n = len(POOL_WINDOWS)
    eye = jnp.eye(n, dtype=pool_w.dtype)
    return jnp.einsum('gcd,gh->gchd', pool_w, eye).reshape(POOL_WIDTH, POOL_WIDTH).astype(BF16)


def kernel(x, mem, positions, ln_g, ln_b, ffn1_w13, ffn1_w2, w_in, pool_w, pool_scale, q_norm_g,
           w_uq, kv_norm_g, w_ukv, w_out, mem_wq, mem_wkv, mem_wo, ffn2_w13, ffn2_w2):
    t_a, t_b = _rope_tables(positions)
    row = lambda v: v[None, :]
    for l in range(DEPTH):
        w_uk, w_uvt = _w_ukv_layout(w_ukv[l])
        x, y_pool, q, k, vt = _pre_attn(
            x, t_a, t_b, ffn1_w13[l].astype(BF16), ffn1_w2[l].astype(BF16), ln_g[l], ln_b[l],
            _w_in_layout(w_in[l]), _pool_w_layout(pool_w[l]), row(pool_scale[l]),
            row(q_norm_g[l]), _w_uq_layout(w_uq[l]), row(kv_norm_g[l]), w_uk, w_uvt)
        y_mla_lo, y_mla_hi = _mla_attn(q, k, vt)
        k_mem, v_mem = _mem_kv(mem, mem_wkv[l].astype(BF16))
        x = _post_attn(x, y_pool, y_mla_lo, y_mla_hi, k_mem, v_mem, w_out[l].astype(BF16),
                       mem_wq[l].astype(BF16), mem_wo[l].astype(BF16), ffn2_w13[l].astype(BF16),
                       ffn2_w2[l].astype(BF16), ln_g[l], ln_b[l])
    return x
```

```python
import functools
import math

import jax
import jax.numpy as jnp
from jax import lax
from jax.experimental import pallas as pl
from jax.experimental.pallas import tpu as pltpu

D_MODEL = 1024
DEPTH = 2
CHUNK = 64
MEM_HEADS = 4
MEM_HEAD_DIM = D_MODEL // MEM_HEADS
POOL_WINDOWS = (2, 4, 8, 16)
POOL_WIDTH = 256
POOL_GROUP = POOL_WIDTH // len(POOL_WINDOWS)
POOL_HALO = 2 * POOL_WINDOWS[-1]
QK_NOPE = 128
QK_ROPE = 64
V_HEAD = 128
MLA_HEADS = 6
QK_HEAD = QK_NOPE + 2 * QK_ROPE
Q_LORA = 256
KV_LORA = 128
ROPE_BASE = 10000.0
D_FF = 2816
ALPHA = (2 * DEPTH) ** 0.25
LN_EPS = 1e-5
RMS_EPS = 1e-6
NEG_INF = -1e30
LOG2_E = math.log2(math.e)

V7X_VMEM_BYTES = 64 * 1024 * 1024
VMEM_LIMIT = 56 * 1024 * 1024

MXU_TILE = 256
FFN_SPLITS = (0, 6 * MXU_TILE, D_FF)
ROW_TILE = 512
ATT_BLOCK = 512
ATT_HEADS = 3

BF16 = jnp.bfloat16
F32 = jnp.float32


def _dot(a, b):
    return jnp.dot(a, b, preferred_element_type=F32)


def _dot_nt(a, b):
    return lax.dot_general(a, b, (((1,), (1,)), ((), ())), preferred_element_type=F32)


def _layer_norm(z, g, b):
    mu = jnp.mean(z, axis=-1, keepdims=True)
    zc = z - mu
    var = jnp.mean(zc * zc, axis=-1, keepdims=True)
    return zc * lax.rsqrt(var + LN_EPS) * g + b


def _rms_norm(z, g):
    return z * lax.rsqrt(jnp.mean(z * z, axis=-1, keepdims=True) + RMS_EPS) * g


def _layer_spec(stacked, layer):
    index = (layer,) + (0,) * (stacked.ndim - 1)
    return pl.BlockSpec((None,) + stacked.shape[1:], lambda *_: index, pipeline_mode=pl.Buffered(1))


def _params(*semantics):
    return pltpu.CompilerParams(dimension_semantics=semantics, vmem_limit_bytes=VMEM_LIMIT)


def _swiglu(xb, w13_ref, w2_ref):
    acc = None
    for c0, c1 in zip(FFN_SPLITS[:-1], FFN_SPLITS[1:]):
        gate = _dot(xb, w13_ref[:, c0:c1])
        up = _dot(xb, w13_ref[:, D_FF + c0:D_FF + c1])
        act = (gate * jax.nn.sigmoid(gate) * up).astype(BF16)
        part = _dot(act, w2_ref[c0:c1, :])
        acc = part if acc is None else acc + part
    return acc


def _rope_table_kernel(pos_ref, freq_ref, ta_ref, tb_ref):
    half = QK_ROPE // 2
    ang = pos_ref[...].astype(F32) * freq_ref[...]
    cos = jnp.cos(ang)
    sin = jnp.sin(ang)
    lane = lax.broadcasted_iota(jnp.int32, (1, 4 * half), 1)
    ta_ref[...] = jnp.where(lane < 2 * half, cos, jnp.where(lane < 3 * half, -sin, sin))
    tb_ref[...] = jnp.where(lane < half, -sin, jnp.where(lane < 2 * half, sin, cos))


def _rope_tables(positions):
    bsz, seq = positions.shape
    half = QK_ROPE // 2
    inv_freq = ROPE_BASE ** (-jnp.arange(half, dtype=F32) / half)
    freq_row = jnp.tile(inv_freq, 4)[None, :]
    table = pl.BlockSpec((None, seq, 4 * half), lambda b: (b, 0, 0))
    return pl.pallas_call(
        _rope_table_kernel,
        grid=(bsz,),
        in_specs=[pl.BlockSpec((None, seq, 1), lambda b: (b, 0, 0)),
                  pl.BlockSpec((1, 4 * half), lambda b: (0, 0))],
        out_specs=[table, table],
        out_shape=[jax.ShapeDtypeStruct((bsz, seq, 4 * half), F32)] * 2,
        compiler_params=_params("parallel"),
        name="rope_tables",
    )(positions[:, :, None], freq_row)


def _pre_attn_kernel(x_ref, ta_ref, tb_ref, w13_ref, w2_ref, g_ref, b_ref, w_in_ref, pool_w_ref,
                     pool_s_ref, qg_ref, w_uq_ref, kvg_ref, w_uk_ref, w_uvt_ref,
                     xo_ref, ypool_ref, q_ref, k_ref, vt_ref, ext_ref):
    step = pl.program_id(1)
    rows = x_ref.shape[0]

    @pl.when(step == 0)
    def _():
        ext_ref[0:POOL_HALO, :] = jnp.zeros((POOL_HALO, POOL_WIDTH), F32)

    x = x_ref[...]
    x = _layer_norm(ALPHA * x + 0.5 * _swiglu(x.astype(BF16), w13_ref, w2_ref),
                    g_ref[0:1, :], b_ref[0:1, :])
    xo_ref[...] = x
    xb = x.astype(BF16)
    h = _dot(xb, w_in_ref[...])
    u = h[:, :POOL_WIDTH]
    c_q = h[:, POOL_WIDTH:POOL_WIDTH + Q_LORA]
    o_kv = POOL_WIDTH + Q_LORA
    c_kv = h[:, o_kv:o_kv + KV_LORA]
    kpe_a = h[:, o_kv + KV_LORA:o_kv + KV_LORA + 128]
    kpe_b = h[:, o_kv + KV_LORA + 128:o_kv + KV_LORA + 256]

    half = POOL_HALO // 2
    ext_ref[POOL_HALO:, :] = u
    lane = lax.broadcasted_iota(jnp.int32, (1, POOL_WIDTH), 1)
    win_sum = None
    shift = 1
    while shift < POOL_WINDOWS[-1]:
        cur = ext_ref[half:, :] + ext_ref[pl.ds(half - shift, rows + half), :]
        shift *= 2
        gi = POOL_WINDOWS.index(shift)
        tile_sum = cur[half:, :]
        win_sum = tile_sum if win_sum is None else jnp.where(lane >= gi * POOL_GROUP, tile_sum, win_sum)
        if shift < POOL_WINDOWS[-1]:
            ext_ref[half:, :] = cur
    ext_ref[0:POOL_HALO, :] = u[rows - POOL_HALO:, :]

    t = step * rows + lax.broadcasted_iota(jnp.int32, (rows, 1), 0)
    win = jnp.left_shift(2, lane // POOL_GROUP)
    cnt = jnp.minimum(t + 1, win).astype(F32)
    d = (win_sum / cnt - u).astype(BF16)
    ypool_ref[...] = (_dot(d, pool_w_ref[...]) * pool_s_ref[...]).astype(BF16)

    t_a = ta_ref[...]
    t_b = tb_ref[...]
    scale = (QK_NOPE + QK_ROPE) ** -0.5 * LOG2_E
    qf = _dot(_rms_norm(c_q, qg_ref[...]).astype(BF16), w_uq_ref[...]) * scale
    kvn = _rms_norm(c_kv, kvg_ref[...]).astype(BF16)
    k_nope = _dot(kvn, w_uk_ref[...])
    vt_ref[...] = _dot_nt(w_uvt_ref[...], kvn).astype(BF16)
    k_rot2 = (kpe_a * t_a + kpe_b * t_b).astype(BF16)
    for hd in range(MLA_HEADS):
        o = hd * QK_HEAD
        q_ref[:, o:o + QK_NOPE] = qf[:, o:o + QK_NOPE].astype(BF16)
        q_ref[:, o + QK_NOPE:o + QK_HEAD] = (qf[:, o + QK_NOPE:o + QK_HEAD] * t_a).astype(BF16)
        k_ref[:, o:o + QK_NOPE] = k_nope[:, hd * QK_NOPE:(hd + 1) * QK_NOPE].astype(BF16)
        k_ref[:, o + QK_NOPE:o + QK_HEAD] = k_rot2


def _pre_attn(layer, x, t_a, t_b, *weights):
    bsz, seq, _ = x.shape
    rows = ROW_TILE
    tile = lambda width: pl.BlockSpec((None, rows, width), lambda b, i: (b, i, 0))
    return pl.pallas_call(
        _pre_attn_kernel,
        grid=(bsz, seq // rows),
        in_specs=[tile(D_MODEL), tile(128), tile(128)] + [_layer_spec(w, layer) for w in weights],
        out_specs=[tile(D_MODEL), tile(POOL_WIDTH), tile(MLA_HEADS * QK_HEAD),
                   tile(MLA_HEADS * QK_HEAD),
                   pl.BlockSpec((None, MLA_HEADS * V_HEAD, rows), lambda b, i: (b, 0, i))],
        out_shape=[jax.ShapeDtypeStruct((bsz, seq, D_MODEL), F32),
                   jax.ShapeDtypeStruct((bsz, seq, POOL_WIDTH), BF16),
                   jax.ShapeDtypeStruct((bsz, seq, MLA_HEADS * QK_HEAD), BF16),
                   jax.ShapeDtypeStruct((bsz, seq, MLA_HEADS * QK_HEAD), BF16),
                   jax.ShapeDtypeStruct((bsz, MLA_HEADS * V_HEAD, seq), BF16)],
        scratch_shapes=[pltpu.VMEM((rows + POOL_HALO, POOL_WIDTH), F32)],
        compiler_params=_params("parallel", "arbitrary"),
        name="pre_attn",
    )(x, t_a, t_b, *weights)


def _mla_attn_kernel(q_lo_ref, q_hi_ref, k_ref, vt_ref, o_lo_ref, o_hi_ref, q2_ref, sa_ref, sb_ref,
                     bmax_a_ref, bmax_b_ref, m_ref, l_ref, acc_ref, *, n_blk):
    blk = ATT_BLOCK
    n_tasks = n_blk + 1
    i = pl.program_id(2)
    n_hi = n_blk - 1 - i

    def task(t):
        if isinstance(t, int) and t == 0:
            return 1, n_hi, True
        if isinstance(t, int) and t == n_tasks - 1:
            return 0, i, True
        u = t - 1
        is_hi = u < n_hi
        return jnp.where(is_hi, 1, 0), jnp.where(is_hi, u, u - n_hi), False

    def scores(t, s_ref, bmax_ref):
        sel, j, diagonal = task(t)
        start = pl.multiple_of(j * blk, blk)
        for hd in range(ATT_HEADS):
            cols = slice(hd * QK_HEAD, (hd + 1) * QK_HEAD)
            s = _dot_nt(k_ref[pl.ds(start, blk), cols], q2_ref[sel, :, cols])
            if diagonal:
                kc = lax.broadcasted_iota(jnp.int32, (blk, blk), 0) // CHUNK
                qc = lax.broadcasted_iota(jnp.int32, (blk, blk), 1) // CHUNK
                s = jnp.where(qc >= kc, s, NEG_INF)
            s_ref[hd] = s
            bmax_ref[hd] = jnp.max(s, axis=0, keepdims=True)

    def update(t, s_ref, bmax_ref):
        sel, j, _ = task(t)
        start = pl.multiple_of(j * blk, blk)
        for hd in range(ATT_HEADS):
            m = m_ref[sel, hd]
            m_new = jnp.maximum(m, bmax_ref[hd])
            alpha = jnp.exp2(m - m_new)
            p = jnp.exp2(s_ref[hd] - m_new)
            m_ref[sel, hd] = m_new
            l_ref[sel, hd] = alpha * l_ref[sel, hd] + jnp.sum(p, axis=0, keepdims=True)
            vt = vt_ref[hd * V_HEAD:(hd + 1) * V_HEAD, pl.ds(start, blk)]
            acc_ref[sel, hd] = alpha * acc_ref[sel, hd] + _dot(vt, p.astype(BF16))

    q2_ref[0] = q_lo_ref[...]
    q2_ref[1] = q_hi_ref[...]
    m_ref[...] = jnp.full(m_ref.shape, NEG_INF, F32)
    l_ref[...] = jnp.zeros(l_ref.shape, F32)
    acc_ref[...] = jnp.zeros(acc_ref.shape, F32)

    buf_a = (sa_ref, bmax_a_ref)
    buf_b = (sb_ref, bmax_b_ref)
    scores(0, *buf_a)
    scores(1, *buf_b)
    update(0, *buf_a)

    def pair(p, _):
        t = 1 + 2 * p
        scores(t + 1, *buf_a)
        update(t, *buf_b)
        scores(t + 2, *buf_b)
        update(t + 1, *buf_a)
        return 0

    lax.fori_loop(0, (n_tasks - 3) // 2, pair, 0)
    scores(n_tasks - 1, *buf_a)
    update(n_tasks - 2, *buf_b)
    update(n_tasks - 1, *buf_a)

    for sel, o_ref in enumerate((o_lo_ref, o_hi_ref)):
        for hd in range(ATT_HEADS):
            out = (acc_ref[sel, hd] / l_ref[sel, hd]).T
            o_ref[:, hd * V_HEAD:(hd + 1) * V_HEAD] = out.astype(BF16)


def _mla_attn(q, k, vt):
    bsz, seq, _ = q.shape
    blk = ATT_BLOCK
    grp = ATT_HEADS
    n_blk = seq // blk
    assert n_blk % 2 == 0
    q_spec = lambda hi: pl.BlockSpec((None, blk, grp * QK_HEAD),
                                     lambda b, h, i: (b, n_blk - 1 - i if hi else i, h))
    o_spec = lambda hi: pl.BlockSpec((None, blk, grp * V_HEAD),
                                     lambda b, h, i: (b, n_blk - 1 - i if hi else i, h))
    out = jax.ShapeDtypeStruct((bsz, seq, MLA_HEADS * V_HEAD), BF16)
    return pl.pallas_call(
        functools.partial(_mla_attn_kernel, n_blk=n_blk),
        grid=(bsz, MLA_HEADS // grp, n_blk // 2),
        in_specs=[q_spec(False), q_spec(True),
                  pl.BlockSpec((None, seq, grp * QK_HEAD), lambda b, h, i: (b, 0, h)),
                  pl.BlockSpec((None, grp * V_HEAD, seq), lambda b, h, i: (b, h, 0))],
        out_specs=[o_spec(False), o_spec(True)],
        out_shape=[out, out],
        scratch_shapes=[pltpu.VMEM((2, blk, grp * QK_HEAD), BF16),
                        pltpu.VMEM((grp, blk, blk), F32), pltpu.VMEM((grp, blk, blk), F32),
                        pltpu.VMEM((grp, 1, blk), F32), pltpu.VMEM((grp, 1, blk), F32),
                        pltpu.VMEM((2, grp, 1, blk), F32), pltpu.VMEM((2, grp, 1, blk), F32),
                        pltpu.VMEM((2, grp, V_HEAD, blk), F32)],
        compiler_params=_params("parallel", "parallel", "arbitrary"),
        name="mla_attn",
    )(q, q, k, vt)


def _mem_weights_kernel(mem_ref, wkv_ref, wq_ref, wo_ref, sw_ref, vw_ref):
    n_mem = mem_ref.shape[0]
    kv = _dot(mem_ref[...].astype(BF16), wkv_ref[...]).astype(BF16)
    scale = MEM_HEAD_DIM ** -0.5 * LOG2_E
    for hd in range(MEM_HEADS):
        sl = slice(hd * MEM_HEAD_DIM, (hd + 1) * MEM_HEAD_DIM)
        k_h = kv[:, hd * MEM_HEAD_DIM:(hd + 1) * MEM_HEAD_DIM]
        v_h = kv[:, D_MODEL + hd * MEM_HEAD_DIM:D_MODEL + (hd + 1) * MEM_HEAD_DIM]
        sw_ref[:, hd * n_mem:(hd + 1) * n_mem] = (_dot_nt(wq_ref[:, sl], k_h) * scale).astype(BF16)
        vw_ref[hd * n_mem:(hd + 1) * n_mem, :] = _dot(v_h, wo_ref[sl, :]).astype(BF16)


def _mem_weights(layer, mem, wkv, wq, wo):
    bsz, n_mem, _ = mem.shape
    width = MEM_HEADS * n_mem
    return pl.pallas_call(
        _mem_weights_kernel,
        grid=(bsz,),
        in_specs=[pl.BlockSpec((None, n_mem, D_MODEL), lambda b: (b, 0, 0))]
        + [_layer_spec(w, layer) for w in (wkv, wq, wo)],
        out_specs=[pl.BlockSpec((None, D_MODEL, width), lambda b: (b, 0, 0)),
                   pl.BlockSpec((None, width, D_MODEL), lambda b: (b, 0, 0))],
        out_shape=[jax.ShapeDtypeStruct((bsz, D_MODEL, width), BF16),
                   jax.ShapeDtypeStruct((bsz, width, D_MODEL), BF16)],
        compiler_params=_params("parallel"),
        name="mem_weights",
    )(mem, wkv, wq, wo)


def _mem_attention(x, sw_ref, vw_ref):
    n_mem = sw_ref.shape[1] // MEM_HEADS
    s = _dot(x.astype(BF16), sw_ref[...])
    probs = []
    for hd in range(MEM_HEADS):
        s_h = s[:, hd * n_mem:(hd + 1) * n_mem]
        p = jnp.exp2(s_h - jnp.max(s_h, axis=-1, keepdims=True))
        probs.append((p / jnp.sum(p, axis=-1, keepdims=True)).astype(BF16))
    return _dot(jnp.concatenate(probs, axis=-1), vw_ref[...])


def _post_attn_kernel(x_ref, yp_ref, ym_lo_ref, ym_hi_ref, sw_ref, vw_ref, w_out_ref,
                      w13_ref, w2_ref, g_ref, b_ref, o_ref):
    ln = lambda z, i: _layer_norm(z, g_ref[i:i + 1, :], b_ref[i:i + 1, :])
    first_half = pl.program_id(1) < pl.num_programs(1) // 2
    ym = jnp.where(first_half, ym_lo_ref[...], ym_hi_ref[...])
    y = _dot(yp_ref[...], w_out_ref[0:POOL_WIDTH, :]) + _dot(ym, w_out_ref[POOL_WIDTH:, :])
    x = ln(ALPHA * x_ref[...] + y, 1)
    x = ln(ALPHA * x + _mem_attention(x, sw_ref, vw_ref), 2)
    x = ln(ALPHA * x + 0.5 * _swiglu(x.astype(BF16), w13_ref, w2_ref), 3)
    o_ref[...] = x


def _post_attn(layer, x, y_pool, y_mla_lo, y_mla_hi, score_w, value_w, *weights):
    bsz, seq, _ = x.shape
    rows = ATT_BLOCK
    half = seq // rows // 2
    tile = lambda width: pl.BlockSpec((None, rows, width), lambda bi, i: (bi, i, 0))
    y_lo = pl.BlockSpec((None, rows, MLA_HEADS * V_HEAD), lambda bi, i: (bi, jnp.minimum(i, half - 1), 0))
    y_hi = pl.BlockSpec((None, rows, MLA_HEADS * V_HEAD), lambda bi, i: (bi, jnp.maximum(i, half), 0))
    per_batch = lambda w: pl.BlockSpec((None,) + w.shape[1:], lambda bi, i: (bi, 0, 0))
    return pl.pallas_call(
        _post_attn_kernel,
        grid=(bsz, seq // rows),
        in_specs=[tile(D_MODEL), tile(POOL_WIDTH), y_lo, y_hi, per_batch(score_w), per_batch(value_w)]
        + [_layer_spec(w, layer) for w in weights],
        out_specs=tile(D_MODEL),
        out_shape=jax.ShapeDtypeStruct((bsz, seq, D_MODEL), F32),
        compiler_params=_params("parallel", "arbitrary"),
        name="post_attn",
    )(x, y_pool, y_mla_lo, y_mla_hi, score_w, value_w, *weights)


def _swap_halves(w):
    half = w.shape[-1] // 2
    return jnp.concatenate([w[..., half:], w[..., :half]], axis=-1)


def _w_in_layout(w_in):
    o_pe = POOL_WIDTH + Q_LORA + KV_LORA
    k_pe = w_in[..., o_pe:]
    k_sw = _swap_halves(k_pe)
    return jnp.concatenate([w_in[..., :o_pe], k_pe, k_sw, k_sw, k_pe], axis=-1).astype(BF16)


def _w_uq_layout(w_uq):
    depth = w_uq.shape[0]
    w = w_uq.reshape(depth, Q_LORA, MLA_HEADS, QK_NOPE + QK_ROPE)
    rope = w[..., QK_NOPE:]
    w = jnp.concatenate([w[..., :QK_NOPE], rope, _swap_halves(rope)], axis=-1)
    return w.reshape(depth, Q_LORA, MLA_HEADS * QK_HEAD).astype(BF16)


def _w_ukv_layout(w_ukv):
    depth = w_ukv.shape[0]
    w = w_ukv.reshape(depth, KV_LORA, MLA_HEADS, QK_NOPE + V_HEAD)
    w_uk = w[..., :QK_NOPE].reshape(depth, KV_LORA, MLA_HEADS * QK_NOPE)
    w_uv = w[..., QK_NOPE:].reshape(depth, KV_LORA, MLA_HEADS * V_HEAD)
    return w_uk.astype(BF16), jnp.swapaxes(w_uv, 1, 2).astype(BF16)


def _pool_w_layout(pool_w):
    n = len(POOL_WINDOWS)
    eye = jnp.eye(n, dtype=pool_w.dtype)
    out = jnp.einsum('lgcd,gh->lgchd', pool_w, eye)
    return out.reshape(pool_w.shape[0], POOL_WIDTH, POOL_WIDTH).astype(BF16)


def kernel(x, mem, positions, ln_g, ln_b, ffn1_w13, ffn1_w2, w_in, pool_w, pool_scale, q_norm_g,
           w_uq, kv_norm_g, w_ukv, w_out, mem_wq, mem_wkv, mem_wo, ffn2_w13, ffn2_w2):
    t_a, t_b = _rope_tables(positions)
    bf16 = lambda w: w.astype(BF16)
    rows = lambda v: v[:, None, :]
    w_uk, w_uvt = _w_ukv_layout(w_ukv)
    pre_w = (bf16(ffn1_w13), bf16(ffn1_w2), ln_g, ln_b, _w_in_layout(w_in), _pool_w_layout(pool_w),
             rows(pool_scale), rows(q_norm_g), _w_uq_layout(w_uq), rows(kv_norm_g), w_uk, w_uvt)
    mem_w = (bf16(mem_wkv), bf16(mem_wq), bf16(mem_wo))
    post_w = (bf16(w_out), bf16(ffn2_w13), bf16(ffn2_w2), ln_g, ln_b)
    for l in range(DEPTH):
        x, y_pool, q, k, vt = _pre_attn(l, x, t_a, t_b, *pre_w)
        y_mla_lo, y_mla_hi = _mla_attn(q, k, vt)
        score_w, value_w = _mem_weights(l, mem, *mem_w)
        x = _post_attn(l, x, y_pool, y_mla_lo, y_mla_hi, score_w, value_w, *post_w)
    return x
```

```python
import functools
import math

import jax
import jax.numpy as jnp
from jax import lax
from jax.experimental import pallas as pl
from jax.experimental.pallas import tpu as pltpu

D_MODEL = 1024
DEPTH = 2
CHUNK = 64
MEM_HEADS = 4
MEM_HEAD_DIM = D_MODEL // MEM_HEADS
POOL_WINDOWS = (2, 4, 8, 16)
POOL_WIDTH = 256
POOL_GROUP = POOL_WIDTH // len(POOL_WINDOWS)
POOL_HALO = 2 * POOL_WINDOWS[-1]
QK_NOPE = 128
QK_ROPE = 64
V_HEAD = 128
MLA_HEADS = 6
QK_HEAD = QK_NOPE + 2 * QK_ROPE
Q_LORA = 256
KV_LORA = 128
ROPE_BASE = 10000.0
D_FF = 2816
ALPHA = (2 * DEPTH) ** 0.25
LN_EPS = 1e-5
RMS_EPS = 1e-6
NEG_INF = -1e30
LOG2_E = math.log2(math.e)

V7X_VMEM_BYTES = 64 * 1024 * 1024
VMEM_LIMIT = 56 * 1024 * 1024

MXU_TILE = 256
FFN_SPLITS = (0, 6 * MXU_TILE, D_FF)
ROW_TILE = 512
ATT_BLOCK = 512
ATT_HEADS = 3

BF16 = jnp.bfloat16
F32 = jnp.float32


def _dot(a, b):
    return jnp.dot(a, b, preferred_element_type=F32)


def _dot_nt(a, b):
    return lax.dot_general(a, b, (((1,), (1,)), ((), ())), preferred_element_type=F32)


def _layer_norm(z, g, b):
    mu = jnp.mean(z, axis=-1, keepdims=True)
    zc = z - mu
    var = jnp.mean(zc * zc, axis=-1, keepdims=True)
    return zc * lax.rsqrt(var + LN_EPS) * g + b


def _rms_norm(z, g):
    return z * lax.rsqrt(jnp.mean(z * z, axis=-1, keepdims=True) + RMS_EPS) * g


def _layer_spec(stacked, layer):
    index = (layer,) + (0,) * (stacked.ndim - 1)
    return pl.BlockSpec((None,) + stacked.shape[1:], lambda *_: index, pipeline_mode=pl.Buffered(1))


def _params(*semantics):
    return pltpu.CompilerParams(dimension_semantics=semantics, vmem_limit_bytes=VMEM_LIMIT)


def _swiglu(xb, w13_ref, w2_ref):
    acc = None
    for c0, c1 in zip(FFN_SPLITS[:-1], FFN_SPLITS[1:]):
        gate = _dot(xb, w13_ref[:, c0:c1])
        up = _dot(xb, w13_ref[:, D_FF + c0:D_FF + c1])
        act = (gate * jax.nn.sigmoid(gate) * up).astype(BF16)
        part = _dot(act, w2_ref[c0:c1, :])
        acc = part if acc is None else acc + part
    return acc


def _rope_table_kernel(pos_ref, freq_ref, ta_ref, tb_ref, tat_ref):
    half = QK_ROPE // 2
    ang = pos_ref[...].astype(F32) * freq_ref[...]
    cos = jnp.cos(ang)
    sin = jnp.sin(ang)
    lane = lax.broadcasted_iota(jnp.int32, (1, 4 * half), 1)
    t_a = jnp.where(lane < 2 * half, cos, jnp.where(lane < 3 * half, -sin, sin))
    ta_ref[...] = t_a
    tb_ref[...] = jnp.where(lane < half, -sin, jnp.where(lane < 2 * half, sin, cos))
    tat_ref[...] = t_a.T


def _rope_tables(positions):
    bsz, seq = positions.shape
    half = QK_ROPE // 2
    inv_freq = ROPE_BASE ** (-jnp.arange(half, dtype=F32) / half)
    freq_row = jnp.tile(inv_freq, 4)[None, :]
    table = pl.BlockSpec((None, seq, 4 * half), lambda b: (b, 0, 0))
    return pl.pallas_call(
        _rope_table_kernel,
        grid=(bsz,),
        in_specs=[pl.BlockSpec((None, seq, 1), lambda b: (b, 0, 0)),
                  pl.BlockSpec((1, 4 * half), lambda b: (0, 0))],
        out_specs=[table, table, pl.BlockSpec((None, 4 * half, seq), lambda b: (b, 0, 0))],
        out_shape=[jax.ShapeDtypeStruct((bsz, seq, 4 * half), F32)] * 2
        + [jax.ShapeDtypeStruct((bsz, 4 * half, seq), F32)],
        compiler_params=_params("parallel"),
        name="rope_tables",
    )(positions[:, :, None], freq_row)


def _pre_attn_kernel(x_ref, ta_ref, tb_ref, tat_ref, w13_ref, w2_ref, g_ref, b_ref, w_in_ref,
                     pool_w_ref, pool_s_ref, qg_ref, w_uqt_ref, kvg_ref, w_uk_ref, w_uvt_ref,
                     xo_ref, ypool_ref, qt_ref, k_ref, vt_ref, ext_ref):
    step = pl.program_id(1)
    rows = x_ref.shape[0]

    @pl.when(step == 0)
    def _():
        ext_ref[0:POOL_HALO, :] = jnp.zeros((POOL_HALO, POOL_WIDTH), F32)

    x = x_ref[...]
    x = _layer_norm(ALPHA * x + 0.5 * _swiglu(x.astype(BF16), w13_ref, w2_ref),
                    g_ref[0:1, :], b_ref[0:1, :])
    xo_ref[...] = x
    xb = x.astype(BF16)
    h = _dot(xb, w_in_ref[...])
    u = h[:, :POOL_WIDTH]
    c_q = h[:, POOL_WIDTH:POOL_WIDTH + Q_LORA]
    o_kv = POOL_WIDTH + Q_LORA
    c_kv = h[:, o_kv:o_kv + KV_LORA]
    kpe_a = h[:, o_kv + KV_LORA:o_kv + KV_LORA + 128]
    kpe_b = h[:, o_kv + KV_LORA + 128:o_kv + KV_LORA + 256]

    half = POOL_HALO // 2
    ext_ref[POOL_HALO:, :] = u
    lane = lax.broadcasted_iota(jnp.int32, (1, POOL_WIDTH), 1)
    win_sum = None
    shift = 1
    while shift < POOL_WINDOWS[-1]:
        cur = ext_ref[half:, :] + ext_ref[pl.ds(half - shift, rows + half), :]
        shift *= 2
        gi = POOL_WINDOWS.index(shift)
        tile_sum = cur[half:, :]
        win_sum = tile_sum if win_sum is None else jnp.where(lane >= gi * POOL_GROUP, tile_sum, win_sum)
        if shift < POOL_WINDOWS[-1]:
            ext_ref[half:, :] = cur
    ext_ref[0:POOL_HALO, :] = u[rows - POOL_HALO:, :]

    t = step * rows + lax.broadcasted_iota(jnp.int32, (rows, 1), 0)
    win = jnp.left_shift(2, lane // POOL_GROUP)
    cnt = jnp.minimum(t + 1, win).astype(F32)
    d = (win_sum / cnt - u).astype(BF16)
    ypool_ref[...] = (_dot(d, pool_w_ref[...]) * pool_s_ref[...]).astype(BF16)

    t_a = ta_ref[...]
    t_b = tb_ref[...]
    scale = (QK_NOPE + QK_ROPE) ** -0.5 * LOG2_E
    qn = _rms_norm(c_q, qg_ref[...]).astype(BF16)
    qt = _dot_nt(w_uqt_ref[...], qn) * scale
    kvn = _rms_norm(c_kv, kvg_ref[...]).astype(BF16)
    k_nope = _dot(kvn, w_uk_ref[...])
    vt_ref[...] = _dot_nt(w_uvt_ref[...], kvn).astype(BF16)
    k_rot2 = (kpe_a * t_a + kpe_b * t_b).astype(BF16)
    t_at = tat_ref[...]
    for hd in range(MLA_HEADS):
        o = hd * QK_HEAD
        qt_ref[o:o + QK_NOPE, :] = qt[o:o + QK_NOPE, :].astype(BF16)
        qt_ref[o + QK_NOPE:o + QK_HEAD, :] = (qt[o + QK_NOPE:o + QK_HEAD, :] * t_at).astype(BF16)
        k_ref[:, o:o + QK_NOPE] = k_nope[:, hd * QK_NOPE:(hd + 1) * QK_NOPE].astype(BF16)
        k_ref[:, o + QK_NOPE:o + QK_HEAD] = k_rot2


def _pre_attn(layer, x, t_a, t_b, t_at, *weights):
    bsz, seq, _ = x.shape
    rows = ROW_TILE
    tile = lambda width: pl.BlockSpec((None, rows, width), lambda b, i: (b, i, 0))
    tile_t = lambda height: pl.BlockSpec((None, height, rows), lambda b, i: (b, 0, i))
    return pl.pallas_call(
        _pre_attn_kernel,
        grid=(bsz, seq // rows),
        in_specs=[tile(D_MODEL), tile(128), tile(128), tile_t(128)]
        + [_layer_spec(w, layer) for w in weights],
        out_specs=[tile(D_MODEL), tile(POOL_WIDTH), tile_t(MLA_HEADS * QK_HEAD),
                   tile(MLA_HEADS * QK_HEAD), tile_t(MLA_HEADS * V_HEAD)],
        out_shape=[jax.ShapeDtypeStruct((bsz, seq, D_MODEL), F32),
                   jax.ShapeDtypeStruct((bsz, seq, POOL_WIDTH), BF16),
                   jax.ShapeDtypeStruct((bsz, MLA_HEADS * QK_HEAD, seq), BF16),
                   jax.ShapeDtypeStruct((bsz, seq, MLA_HEADS * QK_HEAD), BF16),
                   jax.ShapeDtypeStruct((bsz, MLA_HEADS * V_HEAD, seq), BF16)],
        scratch_shapes=[pltpu.VMEM((rows + POOL_HALO, POOL_WIDTH), F32)],
        compiler_params=_params("parallel", "arbitrary"),
        name="pre_attn",
    )(x, t_a, t_b, t_at, *weights)


def _mla_attn_kernel(qt_lo_ref, qt_hi_ref, k_ref, vt_ref, o_ref, q2_ref, sa_ref, sb_ref,
                     bmax_a_ref, bmax_b_ref, m_ref, l_ref, acc_ref, *, n_blk):
    blk = ATT_BLOCK
    n_tasks = n_blk + 1
    i = pl.program_id(2)
    n_hi = n_blk - 1 - i

    def task(t):
        if isinstance(t, int) and t == 0:
            return 1, n_hi, True
        if isinstance(t, int) and t == n_tasks - 1:
            return 0, i, True
        u = t - 1
        is_hi = u < n_hi
        return jnp.where(is_hi, 1, 0), jnp.where(is_hi, u, u - n_hi), False

    def scores(t, s_ref, bmax_ref):
        sel, j, diagonal = task(t)
        start = pl.multiple_of(j * blk, blk)
        for hd in range(ATT_HEADS):
            dims = slice(hd * QK_HEAD, (hd + 1) * QK_HEAD)
            s = _dot(k_ref[pl.ds(start, blk), dims], q2_ref[sel, dims, :])
            if diagonal:
                kc = lax.broadcasted_iota(jnp.int32, (blk, blk), 0) // CHUNK
                qc = lax.broadcasted_iota(jnp.int32, (blk, blk), 1) // CHUNK
                s = jnp.where(qc >= kc, s, NEG_INF)
            s_ref[hd] = s
            bmax_ref[hd] = jnp.max(s, axis=0, keepdims=True)

    def update(t, s_ref, bmax_ref):
        sel, j, _ = task(t)
        start = pl.multiple_of(j * blk, blk)
        for hd in range(ATT_HEADS):
            m = m_ref[sel, hd]
            m_new = jnp.maximum(m, bmax_ref[hd])
            alpha = jnp.exp2(m - m_new)
            p = jnp.exp2(s_ref[hd] - m_new)
            m_ref[sel, hd] = m_new
            l_ref[sel, hd] = alpha * l_ref[sel, hd] + jnp.sum(p, axis=0, keepdims=True)
            vt = vt_ref[hd * V_HEAD:(hd + 1) * V_HEAD, pl.ds(start, blk)]
            acc_ref[sel, hd] = alpha * acc_ref[sel, hd] + _dot(vt, p.astype(BF16))

    q2_ref[0] = qt_lo_ref[...]
    q2_ref[1] = qt_hi_ref[...]
    m_ref[...] = jnp.full(m_ref.shape, NEG_INF, F32)
    l_ref[...] = jnp.zeros(l_ref.shape, F32)
    acc_ref[...] = jnp.zeros(acc_ref.shape, F32)

    buf_a = (sa_ref, bmax_a_ref)
    buf_b = (sb_ref, bmax_b_ref)
    scores(0, *buf_a)
    scores(1, *buf_b)
    update(0, *buf_a)

    def pair(p, _):
        t = 1 + 2 * p
        scores(t + 1, *buf_a)
        update(t, *buf_b)
        scores(t + 2, *buf_b)
        update(t + 1, *buf_a)
        return 0

    lax.fori_loop(0, (n_tasks - 3) // 2, pair, 0)
    scores(n_tasks - 1, *buf_a)
    update(n_tasks - 2, *buf_b)
    update(n_tasks - 1, *buf_a)

    for sel in range(2):
        for hd in range(ATT_HEADS):
            out = (acc_ref[sel, hd] / l_ref[sel, hd]).T
            o_ref[sel, :, hd * V_HEAD:(hd + 1) * V_HEAD] = out.astype(BF16)


def _mla_attn(qt, k, vt):
    bsz, seq, _ = k.shape
    blk = ATT_BLOCK
    grp = ATT_HEADS
    n_blk = seq // blk
    assert n_blk % 2 == 0
    q_spec = lambda hi: pl.BlockSpec((None, grp * QK_HEAD, blk),
                                     lambda b, h, i: (b, h, n_blk - 1 - i if hi else i))
    return pl.pallas_call(
        functools.partial(_mla_attn_kernel, n_blk=n_blk),
        grid=(bsz, MLA_HEADS // grp, n_blk // 2),
        in_specs=[q_spec(False), q_spec(True),
                  pl.BlockSpec((None, seq, grp * QK_HEAD), lambda b, h, i: (b, 0, h)),
                  pl.BlockSpec((None, grp * V_HEAD, seq), lambda b, h, i: (b, h, 0))],
        out_specs=pl.BlockSpec((None, None, 2, blk, grp * V_HEAD), lambda b, h, i: (b, i, 0, 0, h)),
        out_shape=jax.ShapeDtypeStruct((bsz, n_blk // 2, 2, blk, MLA_HEADS * V_HEAD), BF16),
        scratch_shapes=[pltpu.VMEM((2, grp * QK_HEAD, blk), BF16),
                        pltpu.VMEM((grp, blk, blk), F32), pltpu.VMEM((grp, blk, blk), F32),
                        pltpu.VMEM((grp, 1, blk), F32), pltpu.VMEM((grp, 1, blk), F32),
                        pltpu.VMEM((2, grp, 1, blk), F32), pltpu.VMEM((2, grp, 1, blk), F32),
                        pltpu.VMEM((2, grp, V_HEAD, blk), F32)],
        compiler_params=_params("parallel", "parallel", "arbitrary"),
        name="mla_attn",
    )(qt, qt, k, vt)


def _mem_weights_kernel(mem_ref, wkv_ref, wq_ref, wo_ref, sw_ref, vw_ref):
    n_mem = mem_ref.shape[0]
    kv = _dot(mem_ref[...].astype(BF16), wkv_ref[...]).astype(BF16)
    scale = MEM_HEAD_DIM ** -0.5 * LOG2_E
    for hd in range(MEM_HEADS):
        sl = slice(hd * MEM_HEAD_DIM, (hd + 1) * MEM_HEAD_DIM)
        k_h = kv[:, hd * MEM_HEAD_DIM:(hd + 1) * MEM_HEAD_DIM]
        v_h = kv[:, D_MODEL + hd * MEM_HEAD_DIM:D_MODEL + (hd + 1) * MEM_HEAD_DIM]
        sw_ref[:, hd * n_mem:(hd + 1) * n_mem] = (_dot_nt(wq_ref[:, sl], k_h) * scale).astype(BF16)
        vw_ref[hd * n_mem:(hd + 1) * n_mem, :] = _dot(v_h, wo_ref[sl, :]).astype(BF16)


def _mem_weights(layer, mem, wkv, wq, wo):
    bsz, n_mem, _ = mem.shape
    width = MEM_HEADS * n_mem
    return pl.pallas_call(
        _mem_weights_kernel,
        grid=(bsz,),
        in_specs=[pl.BlockSpec((None, n_mem, D_MODEL), lambda b: (b, 0, 0))]
        + [_layer_spec(w, layer) for w in (wkv, wq, wo)],
        out_specs=[pl.BlockSpec((None, D_MODEL, width), lambda b: (b, 0, 0)),
                   pl.BlockSpec((None, width, D_MODEL), lambda b: (b, 0, 0))],
        out_shape=[jax.ShapeDtypeStruct((bsz, D_MODEL, width), BF16),
                   jax.ShapeDtypeStruct((bsz, width, D_MODEL), BF16)],
        compiler_params=_params("parallel"),
        name="mem_weights",
    )(mem, wkv, wq, wo)


def _mem_attention(x, sw_ref, vw_ref):
    n_mem = sw_ref.shape[1] // MEM_HEADS
    s = _dot(x.astype(BF16), sw_ref[...])
    probs = []
    for hd in range(MEM_HEADS):
        s_h = s[:, hd * n_mem:(hd + 1) * n_mem]
        p = jnp.exp2(s_h - jnp.max(s_h, axis=-1, keepdims=True))
        probs.append((p / jnp.sum(p, axis=-1, keepdims=True)).astype(BF16))
    return _dot(jnp.concatenate(probs, axis=-1), vw_ref[...])


def _post_attn_kernel(x_ref, yp_ref, ym_ref, sw_ref, vw_ref, w_out_ref,
                      w13_ref, w2_ref, g_ref, b_ref, o_ref):
    ln = lambda z, i: _layer_norm(z, g_ref[i:i + 1, :], b_ref[i:i + 1, :])
    y = _dot(yp_ref[...], w_out_ref[0:POOL_WIDTH, :]) + _dot(ym_ref[...], w_out_ref[POOL_WIDTH:, :])
    x = ln(ALPHA * x_ref[...] + y, 1)
    x = ln(ALPHA * x + _mem_attention(x, sw_ref, vw_ref), 2)
    x = ln(ALPHA * x + 0.5 * _swiglu(x.astype(BF16), w13_ref, w2_ref), 3)
    o_ref[...] = x


def _post_attn(layer, x, y_pool, y_mla, score_w, value_w, *weights):
    bsz, seq, _ = x.shape
    rows = ATT_BLOCK
    n_blk = seq // rows
    tile = lambda width: pl.BlockSpec((None, rows, width), lambda bi, i: (bi, i, 0))
    y_mla_tile = pl.BlockSpec(
        (None, None, None, rows, MLA_HEADS * V_HEAD),
        lambda bi, i: (bi, jnp.where(i < n_blk // 2, i, n_blk - 1 - i),
                       jnp.where(i < n_blk // 2, 0, 1), 0, 0))
    per_batch = lambda w: pl.BlockSpec((None,) + w.shape[1:], lambda bi, i: (bi, 0, 0))
    return pl.pallas_call(
        _post_attn_kernel,
        grid=(bsz, seq // rows),
        in_specs=[tile(D_MODEL), tile(POOL_WIDTH), y_mla_tile, per_batch(score_w), per_batch(value_w)]
        + [_layer_spec(w, layer) for w in weights],
        out_specs=tile(D_MODEL),
        out_shape=jax.ShapeDtypeStruct((bsz, seq, D_MODEL), F32),
        compiler_params=_params("parallel", "arbitrary"),
        name="post_attn",
    )(x, y_pool, y_mla, score_w, value_w, *weights)


def _swap_halves(w):
    half = w.shape[-1] // 2
    return jnp.concatenate([w[..., half:], w[..., :half]], axis=-1)


def _w_in_layout(w_in):
    o_pe = POOL_WIDTH + Q_LORA + KV_LORA
    k_pe = w_in[..., o_pe:]
    k_sw = _swap_halves(k_pe)
    return jnp.concatenate([w_in[..., :o_pe], k_pe, k_sw, k_sw, k_pe], axis=-1).astype(BF16)


def _w_uq_layout(w_uq):
    depth = w_uq.shape[0]
    w = w_uq.reshape(depth, Q_LORA, MLA_HEADS, QK_NOPE + QK_ROPE)
    rope = w[..., QK_NOPE:]
    w = jnp.concatenate([w[..., :QK_NOPE], rope, _swap_halves(rope)], axis=-1)
    w = w.reshape(depth, Q_LORA, MLA_HEADS * QK_HEAD)
    return jnp.swapaxes(w, 1, 2).astype(BF16)


def _w_ukv_layout(w_ukv):
    depth = w_ukv.shape[0]
    w = w_ukv.reshape(depth, KV_LORA, MLA_HEADS, QK_NOPE + V_HEAD)
    w_uk = w[..., :QK_NOPE].reshape(depth, KV_LORA, MLA_HEADS * QK_NOPE)
    w_uv = w[..., QK_NOPE:].reshape(depth, KV_LORA, MLA_HEADS * V_HEAD)
    return w_uk.astype(BF16), jnp.swapaxes(w_uv, 1, 2).astype(BF16)


def _pool_w_layout(pool_w):
    n = len(POOL_WINDOWS)
    eye = jnp.eye(n, dtype=pool_w.dtype)
    out = jnp.einsum('lgcd,gh->lgchd', pool_w, eye)
    return out.reshape(pool_w.shape[0], POOL_WIDTH, POOL_WIDTH).astype(BF16)


def kernel(x, mem, positions, ln_g, ln_b, ffn1_w13, ffn1_w2, w_in, pool_w, pool_scale, q_norm_g,
           w_uq, kv_norm_g, w_ukv, w_out, mem_wq, mem_wkv, mem_wo, ffn2_w13, ffn2_w2):
    t_a, t_b, t_at = _rope_tables(positions)
    bf16 = lambda w: w.astype(BF16)
    rows = lambda v: v[:, None, :]
    w_uk, w_uvt = _w_ukv_layout(w_ukv)
    pre_w = (bf16(ffn1_w13), bf16(ffn1_w2), ln_g, ln_b, _w_in_layout(w_in), _pool_w_layout(pool_w),
             rows(pool_scale), rows(q_norm_g), _w_uq_layout(w_uq), rows(kv_norm_g), w_uk, w_uvt)
    mem_w = (bf16(mem_wkv), bf16(mem_wq), bf16(mem_wo))
    post_w = (bf16(w_out), bf16(ffn2_w13), bf16(ffn2_w2), ln_g, ln_b)
    for l in range(DEPTH):
        x, y_pool, qt, k, vt = _pre_attn(l, x, t_a, t_b, t_at, *pre_w)
        y_mla = _mla_attn(qt, k, vt)
        score_w, value_w = _mem_weights(l, mem, *mem_w)
        x = _post_attn(l, x, y_pool, y_mla, score_w, value_w, *post_w)
    return x
```

```python
import functools
import math

import jax
import jax.numpy as jnp
from jax import lax
from jax.experimental import pallas as pl
from jax.experimental.pallas import tpu as pltpu

D_MODEL = 1024
DEPTH = 2
CHUNK = 64
MEM_HEADS = 4
MEM_HEAD_DIM = D_MODEL // MEM_HEADS
POOL_WINDOWS = (2, 4, 8, 16)
POOL_WIDTH = 256
POOL_GROUP = POOL_WIDTH // len(POOL_WINDOWS)
POOL_HALO = 2 * POOL_WINDOWS[-1]
QK_NOPE = 128
QK_ROPE = 64
V_HEAD = 128
MLA_HEADS = 6
QK_HEAD = QK_NOPE + 2 * QK_ROPE
Q_LORA = 256
KV_LORA = 128
ROPE_BASE = 10000.0
D_FF = 2816
ALPHA = (2 * DEPTH) ** 0.25
LN_EPS = 1e-5
RMS_EPS = 1e-6
NEG_INF = -1e30
LOG2_E = math.log2(math.e)

V7X_VMEM_BYTES = 64 * 1024 * 1024
VMEM_LIMIT = 56 * 1024 * 1024

MXU_TILE = 256
FFN_SPLITS = (0, 6 * MXU_TILE, D_FF)
ROW_TILE = 512
ATT_BLOCK = 512
ATT_HEADS = 3

BF16 = jnp.bfloat16
F32 = jnp.float32


def _dot(a, b):
    return jnp.dot(a, b, preferred_element_type=F32)


def _dot_nt(a, b):
    return lax.dot_general(a, b, (((1,), (1,)), ((), ())), preferred_element_type=F32)


def _layer_norm(z, g, b):
    mu = jnp.mean(z, axis=-1, keepdims=True)
    zc = z - mu
    var = jnp.mean(zc * zc, axis=-1, keepdims=True)
    return zc * lax.rsqrt(var + LN_EPS) * g + b


def _rms_norm(z, g):
    return z * lax.rsqrt(jnp.mean(z * z, axis=-1, keepdims=True) + RMS_EPS) * g


def _layer_spec(stacked, layer):
    index = (layer,) + (0,) * (stacked.ndim - 1)
    return pl.BlockSpec((None,) + stacked.shape[1:], lambda *_: index, pipeline_mode=pl.Buffered(1))


def _params(*semantics):
    return pltpu.CompilerParams(dimension_semantics=semantics, vmem_limit_bytes=VMEM_LIMIT)


def _swiglu(xb, w13_ref, w2_ref):
    acc = None
    for c0, c1 in zip(FFN_SPLITS[:-1], FFN_SPLITS[1:]):
        gate = _dot(xb, w13_ref[:, c0:c1])
        up = _dot(xb, w13_ref[:, D_FF + c0:D_FF + c1])
        act = (gate * jax.nn.sigmoid(gate) * up).astype(BF16)
        part = _dot(act, w2_ref[c0:c1, :])
        acc = part if acc is None else acc + part
    return acc


def _rope_table_kernel(pos_ref, freq_ref, ta_ref, tat_ref):
    half = QK_ROPE // 2
    ang = pos_ref[...].astype(F32) * freq_ref[...]
    cos = jnp.cos(ang)
    sin = jnp.sin(ang)
    lane = lax.broadcasted_iota(jnp.int32, (1, 4 * half), 1)
    t_a = jnp.where(lane < 2 * half, cos, jnp.where(lane < 3 * half, -sin, sin))
    ta_ref[...] = t_a
    tat_ref[...] = t_a.T


def _rope_tables(positions):
    bsz, seq = positions.shape
    half = QK_ROPE // 2
    inv_freq = ROPE_BASE ** (-jnp.arange(half, dtype=F32) / half)
    freq_row = jnp.tile(inv_freq, 4)[None, :]
    table = pl.BlockSpec((None, seq, 4 * half), lambda b: (b, 0, 0))
    return pl.pallas_call(
        _rope_table_kernel,
        grid=(bsz,),
        in_specs=[pl.BlockSpec((None, seq, 1), lambda b: (b, 0, 0)),
                  pl.BlockSpec((1, 4 * half), lambda b: (0, 0))],
        out_specs=[table, pl.BlockSpec((None, 4 * half, seq), lambda b: (b, 0, 0))],
        out_shape=[jax.ShapeDtypeStruct((bsz, seq, 4 * half), F32),
                   jax.ShapeDtypeStruct((bsz, 4 * half, seq), F32)],
        compiler_params=_params("parallel"),
        name="rope_tables",
    )(positions[:, :, None], freq_row)


def _pre_attn_kernel(x_ref, ta_ref, tat_ref, w13_ref, w2_ref, g_ref, b_ref, w_in_ref,
                     pool_w_ref, pool_s_ref, qg_ref, w_uqt_ref, kvg_ref, w_uk_ref, w_uvt_ref,
                     xo_ref, ypool_ref, qt_ref, k_ref, vt_ref, ext_ref):
    step = pl.program_id(1)
    rows = x_ref.shape[0]

    @pl.when(step == 0)
    def _():
        ext_ref[0:POOL_HALO, :] = jnp.zeros((POOL_HALO, POOL_WIDTH), F32)

    x = x_ref[...]
    x = _layer_norm(ALPHA * x + 0.5 * _swiglu(x.astype(BF16), w13_ref, w2_ref),
                    g_ref[0:1, :], b_ref[0:1, :])
    xo_ref[...] = x
    xb = x.astype(BF16)
    h = _dot(xb, w_in_ref[...])
    u = h[:, :POOL_WIDTH]
    c_q = h[:, POOL_WIDTH:POOL_WIDTH + Q_LORA]
    o_kv = POOL_WIDTH + Q_LORA
    c_kv = h[:, o_kv:o_kv + KV_LORA]
    kpe_a = h[:, o_kv + KV_LORA:o_kv + KV_LORA + 2 * QK_ROPE]
    kpe_b = pltpu.roll(kpe_a, QK_ROPE, axis=1)

    half = POOL_HALO // 2
    ext_ref[POOL_HALO:, :] = u
    lane = lax.broadcasted_iota(jnp.int32, (1, POOL_WIDTH), 1)
    win_sum = None
    shift = 1
    while shift < POOL_WINDOWS[-1]:
        cur = ext_ref[half:, :] + ext_ref[pl.ds(half - shift, rows + half), :]
        shift *= 2
        gi = POOL_WINDOWS.index(shift)
        tile_sum = cur[half:, :]
        win_sum = tile_sum if win_sum is None else jnp.where(lane >= gi * POOL_GROUP, tile_sum, win_sum)
        if shift < POOL_WINDOWS[-1]:
            ext_ref[half:, :] = cur
    ext_ref[0:POOL_HALO, :] = u[rows - POOL_HALO:, :]

    t = step * rows + lax.broadcasted_iota(jnp.int32, (rows, 1), 0)
    win = jnp.left_shift(2, lane // POOL_GROUP)
    cnt = jnp.minimum(t + 1, win).astype(F32)
    d = (win_sum / cnt - u).astype(BF16)
    ypool_ref[...] = (_dot(d, pool_w_ref[...]) * pool_s_ref[...]).astype(BF16)

    t_a = ta_ref[...]
    t_b = pltpu.roll(t_a, QK_ROPE, axis=1)
    scale = (QK_NOPE + QK_ROPE) ** -0.5 * LOG2_E
    qn = _rms_norm(c_q, qg_ref[...]).astype(BF16)
    qt = _dot_nt(w_uqt_ref[...], qn) * scale
    kvn = _rms_norm(c_kv, kvg_ref[...]).astype(BF16)
    k_nope = _dot(kvn, w_uk_ref[...])
    vt_ref[...] = _dot_nt(w_uvt_ref[...], kvn).astype(BF16)
    k_rot2 = (kpe_a * t_a + kpe_b * t_b).astype(BF16)
    t_at = tat_ref[...]
    for hd in range(MLA_HEADS):
        o = hd * QK_HEAD
        qt_ref[o:o + QK_NOPE, :] = qt[o:o + QK_NOPE, :].astype(BF16)
        qt_ref[o + QK_NOPE:o + QK_HEAD, :] = (qt[o + QK_NOPE:o + QK_HEAD, :] * t_at).astype(BF16)
        k_ref[:, o:o + QK_NOPE] = k_nope[:, hd * QK_NOPE:(hd + 1) * QK_NOPE].astype(BF16)
        k_ref[:, o + QK_NOPE:o + QK_HEAD] = k_rot2


def _pre_attn(layer, x, t_a, t_at, *weights):
    bsz, seq, _ = x.shape
    rows = ROW_TILE
    tile = lambda width: pl.BlockSpec((None, rows, width), lambda b, i: (b, i, 0))
    tile_t = lambda height: pl.BlockSpec((None, height, rows), lambda b, i: (b, 0, i))
    return pl.pallas_call(
        _pre_attn_kernel,
        grid=(bsz, seq // rows),
        in_specs=[tile(D_MODEL), tile(128), tile_t(128)] + [_layer_spec(w, layer) for w in weights],
        out_specs=[tile(D_MODEL), tile(POOL_WIDTH), tile_t(MLA_HEADS * QK_HEAD),
                   tile(MLA_HEADS * QK_HEAD), tile_t(MLA_HEADS * V_HEAD)],
        out_shape=[jax.ShapeDtypeStruct((bsz, seq, D_MODEL), F32),
                   jax.ShapeDtypeStruct((bsz, seq, POOL_WIDTH), BF16),
                   jax.ShapeDtypeStruct((bsz, MLA_HEADS * QK_HEAD, seq), BF16),
                   jax.ShapeDtypeStruct((bsz, seq, MLA_HEADS * QK_HEAD), BF16),
                   jax.ShapeDtypeStruct((bsz, MLA_HEADS * V_HEAD, seq), BF16)],
        scratch_shapes=[pltpu.VMEM((rows + POOL_HALO, POOL_WIDTH), F32)],
        compiler_params=_params("parallel", "arbitrary"),
        name="pre_attn",
    )(x, t_a, t_at, *weights)


def _mla_attn_kernel(qt_lo_ref, qt_hi_ref, k_ref, vt_ref, o_ref, q2_ref, sa_ref, sb_ref,
                     bmax_a_ref, bmax_b_ref, m_ref, l_ref, acc_ref, *, n_blk):
    blk = ATT_BLOCK
    n_tasks = n_blk + 1
    i = pl.program_id(2)
    n_hi = n_blk - 1 - i

    def task(t):
        if isinstance(t, int) and t == 0:
            return 1, n_hi, True
        if isinstance(t, int) and t == n_tasks - 1:
            return 0, i, True
        u = t - 1
        is_hi = u < n_hi
        return jnp.where(is_hi, 1, 0), jnp.where(is_hi, u, u - n_hi), False

    def scores(t, s_ref, bmax_ref):
        sel, j, diagonal = task(t)
        start = pl.multiple_of(j * blk, blk)
        for hd in range(ATT_HEADS):
            dims = slice(hd * QK_HEAD, (hd + 1) * QK_HEAD)
            s = _dot(k_ref[pl.ds(start, blk), dims], q2_ref[sel, dims, :])
            if diagonal:
                kc = lax.broadcasted_iota(jnp.int32, (blk, blk), 0) // CHUNK
                qc = lax.broadcasted_iota(jnp.int32, (blk, blk), 1) // CHUNK
                s = jnp.where(qc >= kc, s, NEG_INF)
            s_ref[hd] = s
            bmax_ref[hd] = jnp.max(s, axis=0, keepdims=True)

    def update(t, s_ref, bmax_ref):
        sel, j, _ = task(t)
        start = pl.multiple_of(j * blk, blk)
        for hd in range(ATT_HEADS):
            m = m_ref[sel, hd]
            m_new = jnp.maximum(m, bmax_ref[hd])
            alpha = jnp.exp2(m - m_new)
            p = jnp.exp2(s_ref[hd] - m_new)
            m_ref[sel, hd] = m_new
            l_ref[sel, hd] = alpha * l_ref[sel, hd] + jnp.sum(p, axis=0, keepdims=True)
            vt = vt_ref[hd * V_HEAD:(hd + 1) * V_HEAD, pl.ds(start, blk)]
            acc_ref[sel, hd] = alpha * acc_ref[sel, hd] + _dot(vt, p.astype(BF16))

    q2_ref[0] = qt_lo_ref[...]
    q2_ref[1] = qt_hi_ref[...]
    m_ref[...] = jnp.full(m_ref.shape, NEG_INF, F32)
    l_ref[...] = jnp.zeros(l_ref.shape, F32)
    acc_ref[...] = jnp.zeros(acc_ref.shape, F32)

    buf_a = (sa_ref, bmax_a_ref)
    buf_b = (sb_ref, bmax_b_ref)
    scores(0, *buf_a)
    scores(1, *buf_b)
    update(0, *buf_a)

    def pair(p, _):
        t = 1 + 2 * p
        scores(t + 1, *buf_a)
        update(t, *buf_b)
        scores(t + 2, *buf_b)
        update(t + 1, *buf_a)
        return 0

    lax.fori_loop(0, (n_tasks - 3) // 2, pair, 0)
    scores(n_tasks - 1, *buf_a)
    update(n_tasks - 2, *buf_b)
    update(n_tasks - 1, *buf_a)

    for sel in range(2):
        for hd in range(ATT_HEADS):
            out = (acc_ref[sel, hd] / l_ref[sel, hd]).T
            o_ref[sel, :, hd * V_HEAD:(hd + 1) * V_HEAD] = out.astype(BF16)


def _mla_attn(qt, k, vt):
    bsz, seq, _ = k.shape
    blk = ATT_BLOCK
    grp = ATT_HEADS
    n_blk = seq // blk
    assert n_blk % 2 == 0
    q_spec = lambda hi: pl.BlockSpec((None, grp * QK_HEAD, blk),
                                     lambda b, h, i: (b, h, n_blk - 1 - i if hi else i))
    return pl.pallas_call(
        functools.partial(_mla_attn_kernel, n_blk=n_blk),
        grid=(bsz, MLA_HEADS // grp, n_blk // 2),
        in_specs=[q_spec(False), q_spec(True),
                  pl.BlockSpec((None, seq, grp * QK_HEAD), lambda b, h, i: (b, 0, h)),
                  pl.BlockSpec((None, grp * V_HEAD, seq), lambda b, h, i: (b, h, 0))],
        out_specs=pl.BlockSpec((None, None, 2, blk, grp * V_HEAD), lambda b, h, i: (b, i, 0, 0, h)),
        out_shape=jax.ShapeDtypeStruct((bsz, n_blk // 2, 2, blk, MLA_HEADS * V_HEAD), BF16),
        scratch_shapes=[pltpu.VMEM((2, grp * QK_HEAD, blk), BF16),
                        pltpu.VMEM((grp, blk, blk), F32), pltpu.VMEM((grp, blk, blk), F32),
                        pltpu.VMEM((grp, 1, blk), F32), pltpu.VMEM((grp, 1, blk), F32),
                        pltpu.VMEM((2, grp, 1, blk), F32), pltpu.VMEM((2, grp, 1, blk), F32),
                        pltpu.VMEM((2, grp, V_HEAD, blk), F32)],
        compiler_params=_params("parallel", "parallel", "arbitrary"),
        name="mla_attn",
    )(qt, qt, k, vt)


def _mem_weights_kernel(mem_ref, wkv_ref, wq_ref, wo_ref, sw_ref, vw_ref):
    n_mem = mem_ref.shape[0]
    kv = _dot(mem_ref[...].astype(BF16), wkv_ref[...]).astype(BF16)
    scale = MEM_HEAD_DIM ** -0.5 * LOG2_E
    for hd in range(MEM_HEADS):
        sl = slice(hd * MEM_HEAD_DIM, (hd + 1) * MEM_HEAD_DIM)
        k_h = kv[:, hd * MEM_HEAD_DIM:(hd + 1) * MEM_HEAD_DIM]
        v_h = kv[:, D_MODEL + hd * MEM_HEAD_DIM:D_MODEL + (hd + 1) * MEM_HEAD_DIM]
        sw_ref[:, hd * n_mem:(hd + 1) * n_mem] = (_dot_nt(wq_ref[:, sl], k_h) * scale).astype(BF16)
        vw_ref[hd * n_mem:(hd + 1) * n_mem, :] = _dot(v_h, wo_ref[sl, :]).astype(BF16)


def _mem_weights(layer, mem, wkv, wq, wo):
    bsz, n_mem, _ = mem.shape
    width = MEM_HEADS * n_mem
    return pl.pallas_call(
        _mem_weights_kernel,
        grid=(bsz,),
        in_specs=[pl.BlockSpec((None, n_mem, D_MODEL), lambda b: (b, 0, 0))]
        + [_layer_spec(w, layer) for w in (wkv, wq, wo)],
        out_specs=[pl.BlockSpec((None, D_MODEL, width), lambda b: (b, 0, 0)),
                   pl.BlockSpec((None, width, D_MODEL), lambda b: (b, 0, 0))],
        out_shape=[jax.ShapeDtypeStruct((bsz, D_MODEL, width), BF16),
                   jax.ShapeDtypeStruct((bsz, width, D_MODEL), BF16)],
        compiler_params=_params("parallel"),
        name="mem_weights",
    )(mem, wkv, wq, wo)


def _mem_probs(s):
    n_mem = s.shape[1] // MEM_HEADS
    probs = []
    for hd in range(MEM_HEADS):
        s_h = s[:, hd * n_mem:(hd + 1) * n_mem]
        p = jnp.exp2(s_h - jnp.max(s_h, axis=-1, keepdims=True))
        probs.append((p / jnp.sum(p, axis=-1, keepdims=True)).astype(BF16))
    return jnp.concatenate(probs, axis=-1)


def _post_attn_kernel(x_ref, yp_ref, ym_ref, sw_ref, vw_ref, w_out_ref,
                      w13_ref, w2_ref, g_ref, b_ref, o_ref):
    sub = ATT_BLOCK
    ln = lambda z, i: _layer_norm(z, g_ref[i:i + 1, :], b_ref[i:i + 1, :])
    first_half = pl.program_id(1) < pl.num_programs(1) // 2
    ym_index = (jnp.where(first_half, 0, 1), jnp.where(first_half, 1, 0))
    halves = range(2)
    rows = [slice(h * sub, (h + 1) * sub) for h in halves]
    y = [_dot(yp_ref[rows[h], :], w_out_ref[0:POOL_WIDTH, :])
         + _dot(ym_ref[ym_index[h]], w_out_ref[POOL_WIDTH:, :]) for h in halves]
    x = [ln(ALPHA * x_ref[rows[h], :] + y[h], 1) for h in halves]
    s = [_dot(x[h].astype(BF16), sw_ref[...]) for h in halves]
    p = [_mem_probs(s[h]) for h in halves]
    y = [_dot(p[h], vw_ref[...]) for h in halves]
    x = [ln(ALPHA * x[h] + y[h], 2) for h in halves]
    for h in halves:
        o_ref[rows[h], :] = ln(ALPHA * x[h] + 0.5 * _swiglu(x[h].astype(BF16), w13_ref, w2_ref), 3)


def _post_attn(layer, x, y_pool, y_mla, score_w, value_w, *weights):
    bsz, seq, _ = x.shape
    sub = ATT_BLOCK
    rows = 2 * sub
    n_steps = seq // rows
    tile = lambda width: pl.BlockSpec((None, rows, width), lambda bi, i: (bi, i, 0))
    y_mla_tile = pl.BlockSpec(
        (None, 2, None, sub, MLA_HEADS * V_HEAD),
        lambda bi, i: (bi, jnp.where(i < n_steps // 2, i, n_steps - 1 - i),
                       jnp.where(i < n_steps // 2, 0, 1), 0, 0))
    per_batch = lambda w: pl.BlockSpec((None,) + w.shape[1:], lambda bi, i: (bi, 0, 0),
                                       pipeline_mode=pl.Buffered(1))
    return pl.pallas_call(
        _post_attn_kernel,
        grid=(bsz, n_steps),
        in_specs=[tile(D_MODEL), tile(POOL_WIDTH), y_mla_tile, per_batch(score_w), per_batch(value_w)]
        + [_layer_spec(w, layer) for w in weights],
        out_specs=tile(D_MODEL),
        out_shape=jax.ShapeDtypeStruct((bsz, seq, D_MODEL), F32),
        compiler_params=_params("parallel", "arbitrary"),
        name="post_attn",
    )(x, y_pool, y_mla, score_w, value_w, *weights)


def _swap_halves(w):
    half = w.shape[-1] // 2
    return jnp.concatenate([w[..., half:], w[..., :half]], axis=-1)


def _w_in_layout(w_in):
    o_pe = POOL_WIDTH + Q_LORA + KV_LORA
    k_pe = w_in[..., o_pe:]
    return jnp.concatenate([w_in[..., :o_pe], k_pe, _swap_halves(k_pe)], axis=-1).astype(BF16)


def _w_uq_layout(w_uq):
    depth = w_uq.shape[0]
    w = w_uq.reshape(depth, Q_LORA, MLA_HEADS, QK_NOPE + QK_ROPE)
    rope = w[..., QK_NOPE:]
    w = jnp.concatenate([w[..., :QK_NOPE], rope, _swap_halves(rope)], axis=-1)
    w = w.reshape(depth, Q_LORA, MLA_HEADS * QK_HEAD)
    return jnp.swapaxes(w, 1, 2).astype(BF16)


def _w_ukv_layout(w_ukv):
    depth = w_ukv.shape[0]
    w = w_ukv.reshape(depth, KV_LORA, MLA_HEADS, QK_NOPE + V_HEAD)
    w_uk = w[..., :QK_NOPE].reshape(depth, KV_LORA, MLA_HEADS * QK_NOPE)
    w_uv = w[..., QK_NOPE:].reshape(depth, KV_LORA, MLA_HEADS * V_HEAD)
    return w_uk.astype(BF16), jnp.swapaxes(w_uv, 1, 2).astype(BF16)


def _pool_w_layout(pool_w):
    n = len(POOL_WINDOWS)
    eye = jnp.eye(n, dtype=pool_w.dtype)
    out = jnp.einsum('lgcd,gh->lgchd', pool_w, eye)
    return out.reshape(pool_w.shape[0], POOL_WIDTH, POOL_WIDTH).astype(BF16)


def kernel(x, mem, positions, ln_g, ln_b, ffn1_w13, ffn1_w2, w_in, pool_w, pool_scale, q_norm_g,
           w_uq, kv_norm_g, w_ukv, w_out, mem_wq, mem_wkv, mem_wo, ffn2_w13, ffn2_w2):
    t_a, t_at = _rope_tables(positions)
    bf16 = lambda w: w.astype(BF16)
    rows = lambda v: v[:, None, :]
    w_uk, w_uvt = _w_ukv_layout(w_ukv)
    pre_w = (bf16(ffn1_w13), bf16(ffn1_w2), ln_g, ln_b, _w_in_layout(w_in), _pool_w_layout(pool_w),
             rows(pool_scale), rows(q_norm_g), _w_uq_layout(w_uq), rows(kv_norm_g), w_uk, w_uvt)
    mem_w = (bf16(mem_wkv), bf16(mem_wq), bf16(mem_wo))
    post_w = (bf16(w_out), bf16(ffn2_w13), bf16(ffn2_w2), ln_g, ln_b)
    for l in range(DEPTH):
        x, y_pool, qt, k, vt = _pre_attn(l, x, t_a, t_at, *pre_w)
        y_mla = _mla_attn(qt, k, vt)
        score_w, value_w = _mem_weights(l, mem, *mem_w)
        x = _post_attn(l, x, y_pool, y_mla, score_w, value_w, *post_w)
    return x
```

```python
import functools
import math

import jax
import jax.numpy as jnp
from jax import lax
from jax.experimental import pallas as pl
from jax.experimental.pallas import tpu as pltpu

D_MODEL = 1024
DEPTH = 2
CHUNK = 64
MEM_HEADS = 4
MEM_HEAD_DIM = D_MODEL // MEM_HEADS
POOL_WINDOWS = (2, 4, 8, 16)
POOL_WIDTH = 256
POOL_GROUP = POOL_WIDTH // len(POOL_WINDOWS)
POOL_HALO = 2 * POOL_WINDOWS[-1]
QK_NOPE = 128
QK_ROPE = 64
V_HEAD = 128
MLA_HEADS = 6
QK_HEAD = QK_NOPE + 2 * QK_ROPE
Q_LORA = 256
KV_LORA = 128
ROPE_BASE = 10000.0
D_FF = 2816
ALPHA = (2 * DEPTH) ** 0.25
LN_EPS = 1e-5
RMS_EPS = 1e-6
NEG_INF = -1e30
LOG2_E = math.log2(math.e)

V7X_VMEM_BYTES = 64 * 1024 * 1024
VMEM_LIMIT = V7X_VMEM_BYTES - 4 * 1024 * 1024

MXU_TILE = 256
FFN_SPLITS = (0, 6 * MXU_TILE, D_FF)
ATT_BLOCK = 512
ATT_HEADS = 3

BF16 = jnp.bfloat16
F32 = jnp.float32


def _dot(a, b):
    return jnp.dot(a, b, preferred_element_type=F32)


def _dot_nt(a, b):
    return lax.dot_general(a, b, (((1,), (1,)), ((), ())), preferred_element_type=F32)


def _layer_norm(z, g, b):
    mu = jnp.mean(z, axis=-1, keepdims=True)
    zc = z - mu
    var = jnp.mean(zc * zc, axis=-1, keepdims=True)
    return zc * lax.rsqrt(var + LN_EPS) * g + b


def _rms_norm(z, g):
    return z * lax.rsqrt(jnp.mean(z * z, axis=-1, keepdims=True) + RMS_EPS) * g


def _layer_spec(stacked, layer):
    index = (layer,) + (0,) * (stacked.ndim - 1)
    return pl.BlockSpec((None,) + stacked.shape[1:], lambda *_: index, pipeline_mode=pl.Buffered(1))


def _params(*semantics):
    return pltpu.CompilerParams(dimension_semantics=semantics, vmem_limit_bytes=VMEM_LIMIT)


def _swiglu(xb, w13_ref, w2_ref):
    acc = None
    for c0, c1 in zip(FFN_SPLITS[:-1], FFN_SPLITS[1:]):
        gate = _dot(xb, w13_ref[:, c0:c1])
        up = _dot(xb, w13_ref[:, D_FF + c0:D_FF + c1])
        act = (gate * jax.nn.sigmoid(gate) * up).astype(BF16)
        part = _dot(act, w2_ref[c0:c1, :])
        acc = part if acc is None else acc + part
    return acc


def _rope_table_kernel(pos_ref, freq_ref, ta_ref, tat_ref):
    half = QK_ROPE // 2
    ang = pos_ref[...].astype(F32) * freq_ref[...]
    cos = jnp.cos(ang)
    sin = jnp.sin(ang)
    lane = lax.broadcasted_iota(jnp.int32, (1, 4 * half), 1)
    t_a = jnp.where(lane < 2 * half, cos, jnp.where(lane < 3 * half, -sin, sin))
    ta_ref[...] = t_a
    tat_ref[...] = t_a.T


def _rope_tables(positions):
    bsz, seq = positions.shape
    half = QK_ROPE // 2
    inv_freq = ROPE_BASE ** (-jnp.arange(half, dtype=F32) / half)
    freq_row = jnp.tile(inv_freq, 4)[None, :]
    table = pl.BlockSpec((None, seq, 4 * half), lambda b: (b, 0, 0))
    return pl.pallas_call(
        _rope_table_kernel,
        grid=(bsz,),
        in_specs=[pl.BlockSpec((None, seq, 1), lambda b: (b, 0, 0)),
                  pl.BlockSpec((1, 4 * half), lambda b: (0, 0))],
        out_specs=[table, pl.BlockSpec((None, 4 * half, seq), lambda b: (b, 0, 0))],
        out_shape=[jax.ShapeDtypeStruct((bsz, seq, 4 * half), F32),
                   jax.ShapeDtypeStruct((bsz, 4 * half, seq), F32)],
        compiler_params=_params("parallel"),
        name="rope_tables",
    )(positions[:, :, None], freq_row)


def _pre_attn_kernel(x_ref, ta_ref, tat_ref, w13_ref, w2_ref, g_ref, b_ref, w_in_ref,
                     pool_w_ref, pool_s_ref, qg_ref, w_uqt_ref, kvg_ref, w_uk_ref, w_uvt_ref,
                     xo_ref, ypool_ref, qt_ref, k_ref, vt_ref, ext_ref):
    step = pl.program_id(1)
    rows = x_ref.shape[0]

    @pl.when(step == 0)
    def _():
        ext_ref[0:POOL_HALO, :] = jnp.zeros((POOL_HALO, POOL_WIDTH), F32)

    x = x_ref[...]
    x = _layer_norm(ALPHA * x + 0.5 * _swiglu(x.astype(BF16), w13_ref, w2_ref),
                    g_ref[0:1, :], b_ref[0:1, :])
    xo_ref[...] = x
    xb = x.astype(BF16)
    h = _dot(xb, w_in_ref[...])
    u = h[:, :POOL_WIDTH]
    c_q = h[:, POOL_WIDTH:POOL_WIDTH + Q_LORA]
    o_kv = POOL_WIDTH + Q_LORA
    c_kv = h[:, o_kv:o_kv + KV_LORA]
    kpe_a = h[:, o_kv + KV_LORA:o_kv + KV_LORA + 2 * QK_ROPE]
    kpe_b = pltpu.roll(kpe_a, QK_ROPE, axis=1)

    half = POOL_HALO // 2
    ext_ref[POOL_HALO:, :] = u
    lane = lax.broadcasted_iota(jnp.int32, (1, POOL_WIDTH), 1)
    win_sum = None
    shift = 1
    while shift < POOL_WINDOWS[-1]:
        cur = ext_ref[half:, :] + ext_ref[pl.ds(half - shift, rows + half), :]
        shift *= 2
        gi = POOL_WINDOWS.index(shift)
        tile_sum = cur[half:, :]
        win_sum = tile_sum if win_sum is None else jnp.where(lane >= gi * POOL_GROUP, tile_sum, win_sum)
        if shift < POOL_WINDOWS[-1]:
            ext_ref[half:, :] = cur
    ext_ref[0:POOL_HALO, :] = u[rows - POOL_HALO:, :]

    t = step * rows + lax.broadcasted_iota(jnp.int32, (rows, 1), 0)
    win = jnp.left_shift(2, lane // POOL_GROUP)
    cnt = jnp.minimum(t + 1, win).astype(F32)
    d = (win_sum / cnt - u).astype(BF16)
    ypool_ref[...] = (_dot(d, pool_w_ref[...]) * pool_s_ref[...]).astype(BF16)

    t_a = ta_ref[...]
    t_b = pltpu.roll(t_a, QK_ROPE, axis=1)
    scale = (QK_NOPE + QK_ROPE) ** -0.5 * LOG2_E
    qn = _rms_norm(c_q, qg_ref[...]).astype(BF16)
    qt = _dot_nt(w_uqt_ref[...], qn) * scale
    kvn = _rms_norm(c_kv, kvg_ref[...]).astype(BF16)
    k_nope = _dot(kvn, w_uk_ref[...])
    vt_ref[...] = _dot_nt(w_uvt_ref[...], kvn).astype(BF16)
    k_rot2 = (kpe_a * t_a + kpe_b * t_b).astype(BF16)
    t_at = tat_ref[...]
    for hd in range(MLA_HEADS):
        o = hd * QK_HEAD
        qt_ref[o:o + QK_NOPE, :] = qt[o:o + QK_NOPE, :].astype(BF16)
        qt_ref[o + QK_NOPE:o + QK_HEAD, :] = (qt[o + QK_NOPE:o + QK_HEAD, :] * t_at).astype(BF16)
        k_ref[:, o:o + QK_NOPE] = k_nope[:, hd * QK_NOPE:(hd + 1) * QK_NOPE].astype(BF16)
        k_ref[:, o + QK_NOPE:o + QK_HEAD] = k_rot2


def _pre_attn(layer, x, t_a, t_at, *weights):
    bsz, seq, _ = x.shape
    rows = ATT_BLOCK
    n_blk = seq // rows
    tile = lambda width: pl.BlockSpec((None, rows, width), lambda b, i: (b, i, 0))
    tile_t = lambda height: pl.BlockSpec((None, height, rows), lambda b, i: (b, 0, i))
    qt_tile = pl.BlockSpec(
        (None, None, None, MLA_HEADS * QK_HEAD, rows),
        lambda b, i: (b, jnp.where(i < n_blk // 2, i, n_blk - 1 - i),
                      jnp.where(i < n_blk // 2, 0, 1), 0, 0))
    return pl.pallas_call(
        _pre_attn_kernel,
        grid=(bsz, n_blk),
        in_specs=[tile(D_MODEL), tile(128), tile_t(128)] + [_layer_spec(w, layer) for w in weights],
        out_specs=[tile(D_MODEL), tile(POOL_WIDTH), qt_tile,
                   tile(MLA_HEADS * QK_HEAD), tile_t(MLA_HEADS * V_HEAD)],
        out_shape=[jax.ShapeDtypeStruct((bsz, seq, D_MODEL), F32),
                   jax.ShapeDtypeStruct((bsz, seq, POOL_WIDTH), BF16),
                   jax.ShapeDtypeStruct((bsz, n_blk // 2, 2, MLA_HEADS * QK_HEAD, rows), BF16),
                   jax.ShapeDtypeStruct((bsz, seq, MLA_HEADS * QK_HEAD), BF16),
                   jax.ShapeDtypeStruct((bsz, MLA_HEADS * V_HEAD, seq), BF16)],
        scratch_shapes=[pltpu.VMEM((rows + POOL_HALO, POOL_WIDTH), F32)],
        compiler_params=_params("parallel", "arbitrary"),
        name="pre_attn",
    )(x, t_a, t_at, *weights)


def _mla_attn_kernel(q2_ref, k_ref, vt_ref, o_ref, sa_ref, sb_ref,
                     bmax_a_ref, bmax_b_ref, m_ref, l_ref, acc_ref, *, n_blk):
    blk = ATT_BLOCK
    n_tasks = n_blk + 1
    i = pl.program_id(2)
    n_hi = n_blk - 1 - i

    def task(t):
        if isinstance(t, int) and t == 0:
            return 1, n_hi, True
        if isinstance(t, int) and t == n_tasks - 1:
            return 0, i, True
        u = t - 1
        is_hi = u < n_hi
        return jnp.where(is_hi, 1, 0), jnp.where(is_hi, u, u - n_hi), False

    def scores(t, s_ref, bmax_ref):
        sel, j, diagonal = task(t)
        start = pl.multiple_of(j * blk, blk)
        for hd in range(ATT_HEADS):
            dims = slice(hd * QK_HEAD, (hd + 1) * QK_HEAD)
            s = _dot(k_ref[pl.ds(start, blk), dims], q2_ref[sel, dims, :])
            if diagonal:
                kc = lax.broadcasted_iota(jnp.int32, (blk, blk), 0) // CHUNK
                qc = lax.broadcasted_iota(jnp.int32, (blk, blk), 1) // CHUNK
                s = jnp.where(qc >= kc, s, NEG_INF)
            s_ref[hd] = s
            bmax_ref[hd] = jnp.max(s, axis=0, keepdims=True)

    def update(t, s_ref, bmax_ref):
        sel, j, _ = task(t)
        start = pl.multiple_of(j * blk, blk)
        for hd in range(ATT_HEADS):
            m = m_ref[sel, hd]
            m_new = jnp.maximum(m, bmax_ref[hd])
            alpha = jnp.exp2(m - m_new)
            p = jnp.exp2(s_ref[hd] - m_new)
            m_ref[sel, hd] = m_new
            l_ref[sel, hd] = alpha * l_ref[sel, hd] + jnp.sum(p, axis=0, keepdims=True)
            vt = vt_ref[hd * V_HEAD:(hd + 1) * V_HEAD, pl.ds(start, blk)]
            acc_ref[sel, hd] = alpha * acc_ref[sel, hd] + _dot(vt, p.astype(BF16))

    m_ref[...] = jnp.full(m_ref.shape, NEG_INF, F32)
    l_ref[...] = jnp.zeros(l_ref.shape, F32)
    acc_ref[...] = jnp.zeros(acc_ref.shape, F32)

    buf_a = (sa_ref, bmax_a_ref)
    buf_b = (sb_ref, bmax_b_ref)
    scores(0, *buf_a)
    scores(1, *buf_b)
    update(0, *buf_a)

    def pair(p, _):
        t = 1 + 2 * p
        scores(t + 1, *buf_a)
        update(t, *buf_b)
        scores(t + 2, *buf_b)
        update(t + 1, *buf_a)
        return 0

    lax.fori_loop(0, (n_tasks - 3) // 2, pair, 0)
    scores(n_tasks - 1, *buf_a)
    update(n_tasks - 2, *buf_b)
    update(n_tasks - 1, *buf_a)

    for sel in range(2):
        for hd in range(ATT_HEADS):
            out = (acc_ref[sel, hd] / l_ref[sel, hd]).T
            o_ref[sel, :, hd * V_HEAD:(hd + 1) * V_HEAD] = out.astype(BF16)


def _mla_attn(qt, k, vt):
    bsz, seq, _ = k.shape
    blk = ATT_BLOCK
    grp = ATT_HEADS
    n_blk = seq // blk
    assert n_blk % 2 == 0
    return pl.pallas_call(
        functools.partial(_mla_attn_kernel, n_blk=n_blk),
        grid=(bsz, MLA_HEADS // grp, n_blk // 2),
        in_specs=[pl.BlockSpec((None, None, 2, grp * QK_HEAD, blk), lambda b, h, i: (b, i, 0, h, 0)),
                  pl.BlockSpec((None, seq, grp * QK_HEAD), lambda b, h, i: (b, 0, h)),
                  pl.BlockSpec((None, grp * V_HEAD, seq), lambda b, h, i: (b, h, 0))],
        out_specs=pl.BlockSpec((None, None, 2, blk, grp * V_HEAD), lambda b, h, i: (b, i, 0, 0, h)),
        out_shape=jax.ShapeDtypeStruct((bsz, n_blk // 2, 2, blk, MLA_HEADS * V_HEAD), BF16),
        scratch_shapes=[pltpu.VMEM((grp, blk, blk), F32), pltpu.VMEM((grp, blk, blk), F32),
                        pltpu.VMEM((grp, 1, blk), F32), pltpu.VMEM((grp, 1, blk), F32),
                        pltpu.VMEM((2, grp, 1, blk), F32), pltpu.VMEM((2, grp, 1, blk), F32),
                        pltpu.VMEM((2, grp, V_HEAD, blk), F32)],
        compiler_params=_params("parallel", "parallel", "arbitrary"),
        name="mla_attn",
    )(qt, k, vt)


def _mem_weights_kernel(mem_ref, wkv_ref, wq_ref, wo_ref, sw_ref, vw_ref):
    n_mem = mem_ref.shape[0]
    kv = _dot(mem_ref[...].astype(BF16), wkv_ref[...]).astype(BF16)
    scale = MEM_HEAD_DIM ** -0.5 * LOG2_E
    for hd in range(MEM_HEADS):
        sl = slice(hd * MEM_HEAD_DIM, (hd + 1) * MEM_HEAD_DIM)
        k_h = kv[:, hd * MEM_HEAD_DIM:(hd + 1) * MEM_HEAD_DIM]
        v_h = kv[:, D_MODEL + hd * MEM_HEAD_DIM:D_MODEL + (hd + 1) * MEM_HEAD_DIM]
        sw_ref[:, hd * n_mem:(hd + 1) * n_mem] = (_dot_nt(wq_ref[:, sl], k_h) * scale).astype(BF16)
        vw_ref[hd * n_mem:(hd + 1) * n_mem, :] = _dot(v_h, wo_ref[sl, :]).astype(BF16)


def _mem_weights(layer, mem, wkv, wq, wo):
    bsz, n_mem, _ = mem.shape
    width = MEM_HEADS * n_mem
    return pl.pallas_call(
        _mem_weights_kernel,
        grid=(bsz,),
        in_specs=[pl.BlockSpec((None, n_mem, D_MODEL), lambda b: (b, 0, 0))]
        + [_layer_spec(w, layer) for w in (wkv, wq, wo)],
        out_specs=[pl.BlockSpec((None, D_MODEL, width), lambda b: (b, 0, 0)),
                   pl.BlockSpec((None, width, D_MODEL), lambda b: (b, 0, 0))],
        out_shape=[jax.ShapeDtypeStruct((bsz, D_MODEL, width), BF16),
                   jax.ShapeDtypeStruct((bsz, width, D_MODEL), BF16)],
        compiler_params=_params("parallel"),
        name="mem_weights",
    )(mem, wkv, wq, wo)


def _mem_probs(s):
    n_mem = s.shape[1] // MEM_HEADS
    probs = []
    for hd in range(MEM_HEADS):
        s_h = s[:, hd * n_mem:(hd + 1) * n_mem]
        p = jnp.exp2(s_h - jnp.max(s_h, axis=-1, keepdims=True))
        probs.append((p / jnp.sum(p, axis=-1, keepdims=True)).astype(BF16))
    return jnp.concatenate(probs, axis=-1)


def _post_attn_kernel(x_ref, yp_ref, ym_ref, sw_ref, vw_ref, w_out_ref,
                      w13_ref, w2_ref, g_ref, b_ref, o_ref):
    sub = ATT_BLOCK
    ln = lambda z, i: _layer_norm(z, g_ref[i:i + 1, :], b_ref[i:i + 1, :])
    first_half = pl.program_id(1) < pl.num_programs(1) // 2
    ym_index = (jnp.where(first_half, 0, 1), jnp.where(first_half, 1, 0))
    halves = range(2)
    rows = [slice(h * sub, (h + 1) * sub) for h in halves]
    y = [_dot(yp_ref[rows[h], :], w_out_ref[0:POOL_WIDTH, :])
         + _dot(ym_ref[ym_index[h]], w_out_ref[POOL_WIDTH:, :]) for h in halves]
    x = [ln(ALPHA * x_ref[rows[h], :] + y[h], 1) for h in halves]
    s = [_dot(x[h].astype(BF16), sw_ref[...]) for h in halves]
    p = [_mem_probs(s[h]) for h in halves]
    y = [_dot(p[h], vw_ref[...]) for h in halves]
    x = [ln(ALPHA * x[h] + y[h], 2) for h in halves]
    for h in halves:
        o_ref[rows[h], :] = ln(ALPHA * x[h] + 0.5 * _swiglu(x[h].astype(BF16), w13_ref, w2_ref), 3)


def _post_attn(layer, x, y_pool, y_mla, score_w, value_w, *weights):
    bsz, seq, _ = x.shape
    sub = ATT_BLOCK
    rows = 2 * sub
    n_steps = seq // rows
    tile = lambda width: pl.BlockSpec((None, rows, width), lambda bi, i: (bi, i, 0))
    y_mla_tile = pl.BlockSpec(
        (None, 2, None, sub, MLA_HEADS * V_HEAD),
        lambda bi, i: (bi, jnp.where(i < n_steps // 2, i, n_steps - 1 - i),
                       jnp.where(i < n_steps // 2, 0, 1), 0, 0))
    per_batch = lambda w: pl.BlockSpec((None,) + w.shape[1:], lambda bi, i: (bi, 0, 0))
    return pl.pallas_call(
        _post_attn_kernel,
        grid=(bsz, n_steps),
        in_specs=[tile(D_MODEL), tile(POOL_WIDTH), y_mla_tile, per_batch(score_w), per_batch(value_w)]
        + [_layer_spec(w, layer) for w in weights],
        out_specs=tile(D_MODEL),
        out_shape=jax.ShapeDtypeStruct((bsz, seq, D_MODEL), F32),
        compiler_params=_params("parallel", "arbitrary"),
        name="post_attn",
    )(x, y_pool, y_mla, score_w, value_w, *weights)


def _swap_halves(w):
    half = w.shape[-1] // 2
    return jnp.concatenate([w[..., half:], w[..., :half]], axis=-1)


def _w_in_layout(w_in):
    o_pe = POOL_WIDTH + Q_LORA + KV_LORA
    k_pe = w_in[..., o_pe:]
    return jnp.concatenate([w_in[..., :o_pe], k_pe, _swap_halves(k_pe)], axis=-1).astype(BF16)


def _w_uq_layout(w_uq):
    depth = w_uq.shape[0]
    w = w_uq.reshape(depth, Q_LORA, MLA_HEADS, QK_NOPE + QK_ROPE)
    rope = w[..., QK_NOPE:]
    w = jnp.concatenate([w[..., :QK_NOPE], rope, _swap_halves(rope)], axis=-1)
    w = w.reshape(depth, Q_LORA, MLA_HEADS * QK_HEAD)
    return jnp.swapaxes(w, 1, 2).astype(BF16)


def _w_ukv_layout(w_ukv):
    depth = w_ukv.shape[0]
    w = w_ukv.reshape(depth, KV_LORA, MLA_HEADS, QK_NOPE + V_HEAD)
    w_uk = w[..., :QK_NOPE].reshape(depth, KV_LORA, MLA_HEADS * QK_NOPE)
    w_uv = w[..., QK_NOPE:].reshape(depth, KV_LORA, MLA_HEADS * V_HEAD)
    return w_uk.astype(BF16), jnp.swapaxes(w_uv, 1, 2).astype(BF16)


def _pool_w_layout(pool_w):
    n = len(POOL_WINDOWS)
    eye = jnp.eye(n, dtype=pool_w.dtype)
    out = jnp.einsum('lgcd,gh->lgchd', pool_w, eye)
    return out.reshape(pool_w.shape[0], POOL_WIDTH, POOL_WIDTH).astype(BF16)


def kernel(x, mem, positions, ln_g, ln_b, ffn1_w13, ffn1_w2, w_in, pool_w, pool_scale, q_norm_g,
           w_uq, kv_norm_g, w_ukv, w_out, mem_wq, mem_wkv, mem_wo, ffn2_w13, ffn2_w2):
    t_a, t_at = _rope_tables(positions)
    bf16 = lambda w: w.astype(BF16)
    rows = lambda v: v[:, None, :]
    w_uk, w_uvt = _w_ukv_layout(w_ukv)
    pre_w = (bf16(ffn1_w13), bf16(ffn1_w2), ln_g, ln_b, _w_in_layout(w_in), _pool_w_layout(pool_w),
             rows(pool_scale), rows(q_norm_g), _w_uq_layout(w_uq), rows(kv_norm_g), w_uk, w_uvt)
    mem_w = (bf16(mem_wkv), bf16(mem_wq), bf16(mem_wo))
    post_w = (bf16(w_out), bf16(ffn2_w13), bf16(ffn2_w2), ln_g, ln_b)
    for l in range(DEPTH):
        x, y_pool, qt, k, vt = _pre_attn(l, x, t_a, t_at, *pre_w)
        y_mla = _mla_attn(qt, k, vt)
        score_w, value_w = _mem_weights(l, mem, *mem_w)
        x = _post_attn(l, x, y_pool, y_mla, score_w, value_w, *post_w)
    return x
```

```python
import functools
import math

import jax
import jax.numpy as jnp
from jax import lax
from jax.experimental import pallas as pl
from jax.experimental.pallas import tpu as pltpu

D_MODEL = 1024
DEPTH = 2
CHUNK = 64
MEM_HEADS = 4
MEM_HEAD_DIM = D_MODEL // MEM_HEADS
POOL_WINDOWS = (2, 4, 8, 16)
POOL_WIDTH = 256
POOL_GROUP = POOL_WIDTH // len(POOL_WINDOWS)
POOL_HALO = 2 * POOL_WINDOWS[-1]
QK_NOPE = 128
QK_ROPE = 64
V_HEAD = 128
MLA_HEADS = 6
QK_HEAD = QK_NOPE + 2 * QK_ROPE
Q_LORA = 256
KV_LORA = 128
ROPE_BASE = 10000.0
D_FF = 2816
ALPHA = (2 * DEPTH) ** 0.25
LN_EPS = 1e-5
RMS_EPS = 1e-6
NEG_INF = -1e30
LOG2_E = math.log2(math.e)

V7X_VMEM_BYTES = 64 * 1024 * 1024
VMEM_LIMIT = V7X_VMEM_BYTES - 4 * 1024 * 1024

MXU_TILE = 256
FFN_SPLITS = (0, 6 * MXU_TILE, D_FF)
ATT_BLOCK = 512
ATT_HEADS = 3

BF16 = jnp.bfloat16
F32 = jnp.float32


def _dot(a, b):
    return jnp.dot(a, b, preferred_element_type=F32)


def _dot_nt(a, b):
    return lax.dot_general(a, b, (((1,), (1,)), ((), ())), preferred_element_type=F32)


def _layer_norm(z, g, b):
    mu = jnp.mean(z, axis=-1, keepdims=True)
    zc = z - mu
    var = jnp.mean(zc * zc, axis=-1, keepdims=True)
    return zc * lax.rsqrt(var + LN_EPS) * g + b


def _rms_norm(z, g):
    return z * lax.rsqrt(jnp.mean(z * z, axis=-1, keepdims=True) + RMS_EPS) * g


def _layer_spec(stacked, layer):
    index = (layer,) + (0,) * (stacked.ndim - 1)
    return pl.BlockSpec((None,) + stacked.shape[1:], lambda *_: index, pipeline_mode=pl.Buffered(1))


def _params(*semantics):
    return pltpu.CompilerParams(dimension_semantics=semantics, vmem_limit_bytes=VMEM_LIMIT)


def _swiglu(xb, w13_ref, w2_ref):
    acc = None
    for c0, c1 in zip(FFN_SPLITS[:-1], FFN_SPLITS[1:]):
        gate = _dot(xb, w13_ref[:, c0:c1])
        up = _dot(xb, w13_ref[:, D_FF + c0:D_FF + c1])
        act = (gate * jax.nn.sigmoid(gate) * up).astype(BF16)
        part = _dot(act, w2_ref[c0:c1, :])
        acc = part if acc is None else acc + part
    return acc


def _rope_table_kernel(pos_ref, freq_ref, ta_ref, tat_ref):
    half = QK_ROPE // 2
    ang = pos_ref[...].astype(F32) * freq_ref[...]
    cos = jnp.cos(ang)
    sin = jnp.sin(ang)
    lane = lax.broadcasted_iota(jnp.int32, (1, 4 * half), 1)
    t_a = jnp.where(lane < 2 * half, cos, jnp.where(lane < 3 * half, -sin, sin))
    ta_ref[...] = t_a
    tat_ref[...] = t_a.T


def _rope_tables(positions):
    bsz, seq = positions.shape
    half = QK_ROPE // 2
    inv_freq = ROPE_BASE ** (-jnp.arange(half, dtype=F32) / half)
    freq_row = jnp.tile(inv_freq, 4)[None, :]
    table = pl.BlockSpec((None, seq, 4 * half), lambda b: (b, 0, 0))
    return pl.pallas_call(
        _rope_table_kernel,
        grid=(bsz,),
        in_specs=[pl.BlockSpec((None, seq, 1), lambda b: (b, 0, 0)),
                  pl.BlockSpec((1, 4 * half), lambda b: (0, 0))],
        out_specs=[table, pl.BlockSpec((None, 4 * half, seq), lambda b: (b, 0, 0))],
        out_shape=[jax.ShapeDtypeStruct((bsz, seq, 4 * half), F32),
                   jax.ShapeDtypeStruct((bsz, 4 * half, seq), F32)],
        compiler_params=_params("parallel"),
        name="rope_tables",
    )(positions[:, :, None], freq_row)


def _pre_attn_kernel(x_ref, ta_ref, tat_ref, w13_ref, w2_ref, g_ref, b_ref, w_in_ref,
                     pool_w_ref, pool_s_ref, qg_ref, w_uqt_ref, kvg_ref, w_uk_ref, w_uvt_ref,
                     xo_ref, ypool_ref, qt_ref, k_ref, vt_ref, ext_ref):
    step = pl.program_id(1)
    rows = x_ref.shape[0]

    @pl.when(step == 0)
    def _():
        ext_ref[0:POOL_HALO, :] = jnp.zeros((POOL_HALO, POOL_WIDTH), F32)

    x = x_ref[...]
    x = _layer_norm(ALPHA * x + 0.5 * _swiglu(x.astype(BF16), w13_ref, w2_ref),
                    g_ref[0:1, :], b_ref[0:1, :])
    xo_ref[...] = x
    xb = x.astype(BF16)
    h = _dot(xb, w_in_ref[...])
    u = h[:, :POOL_WIDTH]
    c_q = h[:, POOL_WIDTH:POOL_WIDTH + Q_LORA]
    o_kv = POOL_WIDTH + Q_LORA
    c_kv = h[:, o_kv:o_kv + KV_LORA]
    kpe_a = h[:, o_kv + KV_LORA:o_kv + KV_LORA + 2 * QK_ROPE]
    kpe_b = pltpu.roll(kpe_a, QK_ROPE, axis=1)

    half = POOL_HALO // 2
    ext_ref[POOL_HALO:, :] = u
    lane = lax.broadcasted_iota(jnp.int32, (1, POOL_WIDTH), 1)
    win_sum = None
    shift = 1
    while shift < POOL_WINDOWS[-1]:
        cur = ext_ref[half:, :] + ext_ref[pl.ds(half - shift, rows + half), :]
        shift *= 2
        gi = POOL_WINDOWS.index(shift)
        tile_sum = cur[half:, :]
        win_sum = tile_sum if win_sum is None else jnp.where(lane >= gi * POOL_GROUP, tile_sum, win_sum)
        if shift < POOL_WINDOWS[-1]:
            ext_ref[half:, :] = cur
    ext_ref[0:POOL_HALO, :] = u[rows - POOL_HALO:, :]

    t = step * rows + lax.broadcasted_iota(jnp.int32, (rows, 1), 0)
    win = jnp.left_shift(2, lane // POOL_GROUP)
    cnt = jnp.minimum(t + 1, win).astype(F32)
    d = (win_sum / cnt - u).astype(BF16)
    ypool_ref[...] = (_dot(d, pool_w_ref[...]) * pool_s_ref[...]).astype(BF16)

    t_a = ta_ref[...]
    t_b = pltpu.roll(t_a, QK_ROPE, axis=1)
    scale = (QK_NOPE + QK_ROPE) ** -0.5 * LOG2_E
    qn = _rms_norm(c_q, qg_ref[...]).astype(BF16)
    qt = _dot_nt(w_uqt_ref[...], qn) * scale
    kvn = _rms_norm(c_kv, kvg_ref[...]).astype(BF16)
    k_nope = _dot(kvn, w_uk_ref[...])
    vt_ref[...] = _dot_nt(w_uvt_ref[...], kvn).astype(BF16)
    k_rot2 = (kpe_a * t_a + kpe_b * t_b).astype(BF16)
    t_at = tat_ref[...]
    for hd in range(MLA_HEADS):
        o = hd * QK_HEAD
        qt_ref[o:o + QK_NOPE, :] = qt[o:o + QK_NOPE, :].astype(BF16)
        qt_ref[o + QK_NOPE:o + QK_HEAD, :] = (qt[o + QK_NOPE:o + QK_HEAD, :] * t_at).astype(BF16)
        k_ref[:, o:o + QK_NOPE] = k_nope[:, hd * QK_NOPE:(hd + 1) * QK_NOPE].astype(BF16)
        k_ref[:, o + QK_NOPE:o + QK_HEAD] = k_rot2


def _pre_attn(layer, x, t_a, t_at, *weights):
    bsz, seq, _ = x.shape
    rows = ATT_BLOCK
    n_blk = seq // rows
    tile = lambda width: pl.BlockSpec((None, rows, width), lambda b, i: (b, i, 0))
    tile_t = lambda height: pl.BlockSpec((None, height, rows), lambda b, i: (b, 0, i))
    qt_tile = pl.BlockSpec(
        (None, None, None, MLA_HEADS * QK_HEAD, rows),
        lambda b, i: (b, jnp.where(i < n_blk // 2, i, n_blk - 1 - i),
                      jnp.where(i < n_blk // 2, 0, 1), 0, 0))
    return pl.pallas_call(
        _pre_attn_kernel,
        grid=(bsz, n_blk),
        in_specs=[tile(D_MODEL), tile(128), tile_t(128)] + [_layer_spec(w, layer) for w in weights],
        out_specs=[tile(D_MODEL), tile(POOL_WIDTH), qt_tile,
                   tile(MLA_HEADS * QK_HEAD), tile_t(MLA_HEADS * V_HEAD)],
        out_shape=[jax.ShapeDtypeStruct((bsz, seq, D_MODEL), F32),
                   jax.ShapeDtypeStruct((bsz, seq, POOL_WIDTH), BF16),
                   jax.ShapeDtypeStruct((bsz, n_blk // 2, 2, MLA_HEADS * QK_HEAD, rows), BF16),
                   jax.ShapeDtypeStruct((bsz, seq, MLA_HEADS * QK_HEAD), BF16),
                   jax.ShapeDtypeStruct((bsz, MLA_HEADS * V_HEAD, seq), BF16)],
        scratch_shapes=[pltpu.VMEM((rows + POOL_HALO, POOL_WIDTH), F32)],
        compiler_params=_params("parallel", "arbitrary"),
        name="pre_attn",
    )(x, t_a, t_at, *weights)


def _mla_attn_kernel(q2_ref, k_ref, vt_ref, o_ref, sa_ref, sb_ref,
                     bmax_a_ref, bmax_b_ref, m_ref, l_ref, acc_ref, *, n_blk):
    blk = ATT_BLOCK
    n_tasks = n_blk + 1
    i = pl.program_id(2)
    n_hi = n_blk - 1 - i

    def task(t):
        if isinstance(t, int) and t == 0:
            return 1, n_hi, True
        if isinstance(t, int) and t == n_tasks - 1:
            return 0, i, True
        u = t - 1
        is_hi = u < n_hi
        return jnp.where(is_hi, 1, 0), jnp.where(is_hi, u, u - n_hi), False

    def scores(t, s_ref, bmax_ref):
        sel, j, diagonal = task(t)
        start = pl.multiple_of(j * blk, blk)
        for hd in range(ATT_HEADS):
            dims = slice(hd * QK_HEAD, (hd + 1) * QK_HEAD)
            s = _dot(k_ref[pl.ds(start, blk), dims], q2_ref[sel, dims, :])
            if diagonal:
                kc = lax.broadcasted_iota(jnp.int32, (blk, blk), 0) // CHUNK
                qc = lax.broadcasted_iota(jnp.int32, (blk, blk), 1) // CHUNK
                s = jnp.where(qc >= kc, s, NEG_INF)
            s_ref[hd] = s
            bmax_ref[hd] = jnp.max(s, axis=0, keepdims=True)

    def update(t, s_ref, bmax_ref):
        sel, j, _ = task(t)
        start = pl.multiple_of(j * blk, blk)
        for hd in range(ATT_HEADS):
            vt = vt_ref[hd * V_HEAD:(hd + 1) * V_HEAD, pl.ds(start, blk)]
            for q0 in range(0, blk, MXU_TILE):
                ql = slice(q0, q0 + MXU_TILE)
                m = m_ref[sel, hd, :, ql]
                m_new = jnp.maximum(m, bmax_ref[hd, :, ql])
                alpha = jnp.exp2(m - m_new)
                p = jnp.exp2(s_ref[hd, :, ql] - m_new)
                m_ref[sel, hd, :, ql] = m_new
                l_ref[sel, hd, :, ql] = alpha * l_ref[sel, hd, :, ql] + jnp.sum(p, axis=0, keepdims=True)
                acc_ref[sel, hd, :, ql] = alpha * acc_ref[sel, hd, :, ql] + _dot(vt, p.astype(BF16))

    m_ref[...] = jnp.full(m_ref.shape, NEG_INF, F32)
    l_ref[...] = jnp.zeros(l_ref.shape, F32)
    acc_ref[...] = jnp.zeros(acc_ref.shape, F32)

    buf_a = (sa_ref, bmax_a_ref)
    buf_b = (sb_ref, bmax_b_ref)
    scores(0, *buf_a)
    scores(1, *buf_b)
    update(0, *buf_a)

    def pair(p, _):
        t = 1 + 2 * p
        scores(t + 1, *buf_a)
        update(t, *buf_b)
        scores(t + 2, *buf_b)
        update(t + 1, *buf_a)
        return 0

    lax.fori_loop(0, (n_tasks - 3) // 2, pair, 0)
    scores(n_tasks - 1, *buf_a)
    update(n_tasks - 2, *buf_b)
    update(n_tasks - 1, *buf_a)

    for sel in range(2):
        for hd in range(ATT_HEADS):
            out = (acc_ref[sel, hd] / l_ref[sel, hd]).T
            o_ref[sel, :, hd * V_HEAD:(hd + 1) * V_HEAD] = out.astype(BF16)


def _mla_attn(qt, k, vt):
    bsz, seq, _ = k.shape
    blk = ATT_BLOCK
    grp = ATT_HEADS
    n_blk = seq // blk
    assert n_blk % 2 == 0
    return pl.pallas_call(
        functools.partial(_mla_attn_kernel, n_blk=n_blk),
        grid=(bsz, MLA_HEADS // grp, n_blk // 2),
        in_specs=[pl.BlockSpec((None, None, 2, grp * QK_HEAD, blk), lambda b, h, i: (b, i, 0, h, 0)),
                  pl.BlockSpec((None, seq, grp * QK_HEAD), lambda b, h, i: (b, 0, h)),
                  pl.BlockSpec((None, grp * V_HEAD, seq), lambda b, h, i: (b, h, 0))],
        out_specs=pl.BlockSpec((None, None, 2, blk, grp * V_HEAD), lambda b, h, i: (b, i, 0, 0, h)),
        out_shape=jax.ShapeDtypeStruct((bsz, n_blk // 2, 2, blk, MLA_HEADS * V_HEAD), BF16),
        scratch_shapes=[pltpu.VMEM((grp, blk, blk), F32), pltpu.VMEM((grp, blk, blk), F32),
                        pltpu.VMEM((grp, 1, blk), F32), pltpu.VMEM((grp, 1, blk), F32),
                        pltpu.VMEM((2, grp, 1, blk), F32), pltpu.VMEM((2, grp, 1, blk), F32),
                        pltpu.VMEM((2, grp, V_HEAD, blk), F32)],
        compiler_params=_params("parallel", "parallel", "arbitrary"),
        name="mla_attn",
    )(qt, k, vt)


def _mem_weights_kernel(mem_ref, wkv_ref, wq_ref, wo_ref, sw_ref, vw_ref):
    n_mem = mem_ref.shape[0]
    kv = _dot(mem_ref[...].astype(BF16), wkv_ref[...]).astype(BF16)
    scale = MEM_HEAD_DIM ** -0.5 * LOG2_E
    for hd in range(MEM_HEADS):
        sl = slice(hd * MEM_HEAD_DIM, (hd + 1) * MEM_HEAD_DIM)
        k_h = kv[:, hd * MEM_HEAD_DIM:(hd + 1) * MEM_HEAD_DIM]
        v_h = kv[:, D_MODEL + hd * MEM_HEAD_DIM:D_MODEL + (hd + 1) * MEM_HEAD_DIM]
        sw_ref[:, hd * n_mem:(hd + 1) * n_mem] = (_dot_nt(wq_ref[:, sl], k_h) * scale).astype(BF16)
        vw_ref[hd * n_mem:(hd + 1) * n_mem, :] = _dot(v_h, wo_ref[sl, :]).astype(BF16)


def _mem_weights(layer, mem, wkv, wq, wo):
    bsz, n_mem, _ = mem.shape
    width = MEM_HEADS * n_mem
    return pl.pallas_call(
        _mem_weights_kernel,
        grid=(bsz,),
        in_specs=[pl.BlockSpec((None, n_mem, D_MODEL), lambda b: (b, 0, 0))]
        + [_layer_spec(w, layer) for w in (wkv, wq, wo)],
        out_specs=[pl.BlockSpec((None, D_MODEL, width), lambda b: (b, 0, 0)),
                   pl.BlockSpec((None, width, D_MODEL), lambda b: (b, 0, 0))],
        out_shape=[jax.ShapeDtypeStruct((bsz, D_MODEL, width), BF16),
                   jax.ShapeDtypeStruct((bsz, width, D_MODEL), BF16)],
        compiler_params=_params("parallel"),
        name="mem_weights",
    )(mem, wkv, wq, wo)


def _mem_probs(s):
    n_mem = s.shape[1] // MEM_HEADS
    probs = []
    for hd in range(MEM_HEADS):
        s_h = s[:, hd * n_mem:(hd + 1) * n_mem]
        p = jnp.exp2(s_h - jnp.max(s_h, axis=-1, keepdims=True))
        probs.append((p / jnp.sum(p, axis=-1, keepdims=True)).astype(BF16))
    return jnp.concatenate(probs, axis=-1)


def _post_attn_kernel(x_ref, yp_ref, ym_ref, sw_ref, vw_ref, w_out_ref,
                      w13_ref, w2_ref, g_ref, b_ref, o_ref):
    sub = ATT_BLOCK
    ln = lambda z, i: _layer_norm(z, g_ref[i:i + 1, :], b_ref[i:i + 1, :])
    first_half = pl.program_id(1) < pl.num_programs(1) // 2
    ym_index = (jnp.where(first_half, 0, 1), jnp.where(first_half, 1, 0))
    halves = range(2)
    rows = [slice(h * sub, (h + 1) * sub) for h in halves]
    y = [_dot(yp_ref[rows[h], :], w_out_ref[0:POOL_WIDTH, :])
         + _dot(ym_ref[ym_index[h]], w_out_ref[POOL_WIDTH:, :]) for h in halves]
    x = [ln(ALPHA * x_ref[rows[h], :] + y[h], 1) for h in halves]
    s = [_dot(x[h].astype(BF16), sw_ref[...]) for h in halves]
    p = [_mem_probs(s[h]) for h in halves]
    y = [_dot(p[h], vw_ref[...]) for h in halves]
    x = [ln(ALPHA * x[h] + y[h], 2) for h in halves]
    for h in halves:
        o_ref[rows[h], :] = ln(ALPHA * x[h] + 0.5 * _swiglu(x[h].astype(BF16), w13_ref, w2_ref), 3)


def _post_attn(layer, x, y_pool, y_mla, score_w, value_w, *weights):
    bsz, seq, _ = x.shape
    sub = ATT_BLOCK
    rows = 2 * sub
    n_steps = seq // rows
    tile = lambda width: pl.BlockSpec((None, rows, width), lambda bi, i: (bi, i, 0))
    y_mla_tile = pl.BlockSpec(
        (None, 2, None, sub, MLA_HEADS * V_HEAD),
        lambda bi, i: (bi, jnp.where(i < n_steps // 2, i, n_steps - 1 - i),
                       jnp.where(i < n_steps // 2, 0, 1), 0, 0))
    per_batch = lambda w: pl.BlockSpec((None,) + w.shape[1:], lambda bi, i: (bi, 0, 0))
    return pl.pallas_call(
        _post_attn_kernel,
        grid=(bsz, n_steps),
        in_specs=[tile(D_MODEL), tile(POOL_WIDTH), y_mla_tile, per_batch(score_w), per_batch(value_w)]
        + [_layer_spec(w, layer) for w in weights],
        out_specs=tile(D_MODEL),
        out_shape=jax.ShapeDtypeStruct((bsz, seq, D_MODEL), F32),
        compiler_params=_params("parallel", "arbitrary"),
        name="post_attn",
    )(x, y_pool, y_mla, score_w, value_w, *weights)


def _swap_halves(w):
    half = w.shape[-1] // 2
    return jnp.concatenate([w[..., half:], w[..., :half]], axis=-1)


def _w_in_layout(w_in):
    o_pe = POOL_WIDTH + Q_LORA + KV_LORA
    k_pe = w_in[..., o_pe:]
    return jnp.concatenate([w_in[..., :o_pe], k_pe, _swap_halves(k_pe)], axis=-1).astype(BF16)


def _w_uq_layout(w_uq):
    depth = w_uq.shape[0]
    w = w_uq.reshape(depth, Q_LORA, MLA_HEADS, QK_NOPE + QK_ROPE)
    rope = w[..., QK_NOPE:]
    w = jnp.concatenate([w[..., :QK_NOPE], rope, _swap_halves(rope)], axis=-1)
    w = w.reshape(depth, Q_LORA, MLA_HEADS * QK_HEAD)
    return jnp.swapaxes(w, 1, 2).astype(BF16)


def _w_ukv_layout(w_ukv):
    depth = w_ukv.shape[0]
    w = w_ukv.reshape(depth, KV_LORA, MLA_HEADS, QK_NOPE + V_HEAD)
    w_uk = w[..., :QK_NOPE].reshape(depth, KV_LORA, MLA_HEADS * QK_NOPE)
    w_uv = w[..., QK_NOPE:].reshape(depth, KV_LORA, MLA_HEADS * V_HEAD)
    return w_uk.astype(BF16), jnp.swapaxes(w_uv, 1, 2).astype(BF16)


def _pool_w_layout(pool_w):
    n = len(POOL_WINDOWS)
    eye = jnp.eye(n, dtype=pool_w.dtype)
    out = jnp.einsum('lgcd,gh->lgchd', pool_w, eye)
    return out.reshape(pool_w.shape[0], POOL_WIDTH, POOL_WIDTH).astype(BF16)


def kernel(x, mem, positions, ln_g, ln_b, ffn1_w13, ffn1_w2, w_in, pool_w, pool_scale, q_norm_g,
           w_uq, kv_norm_g, w_ukv, w_out, mem_wq, mem_wkv, mem_wo, ffn2_w13, ffn2_w2):
    t_a, t_at = _rope_tables(positions)
    bf16 = lambda w: w.astype(BF16)
    rows = lambda v: v[:, None, :]
    w_uk, w_uvt = _w_ukv_layout(w_ukv)
    pre_w = (bf16(ffn1_w13), bf16(ffn1_w2), ln_g, ln_b, _w_in_layout(w_in), _pool_w_layout(pool_w),
             rows(pool_scale), rows(q_norm_g), _w_uq_layout(w_uq), rows(kv_norm_g), w_uk, w_uvt)
    mem_w = (bf16(mem_wkv), bf16(mem_wq), bf16(mem_wo))
    post_w = (bf16(w_out), bf16(ffn2_w13), bf16(ffn2_w2), ln_g, ln_b)
    for l in range(DEPTH):
        x, y_pool, qt, k, vt = _pre_attn(l, x, t_a, t_at, *pre_w)
        y_mla = _mla_attn(qt, k, vt)
        score_w, value_w = _mem_weights(l, mem, *mem_w)
        x = _post_attn(l, x, y_pool, y_mla, score_w, value_w, *post_w)
    return x
```
